```python
import jax, jax.numpy as jnp
from jax import lax
import numpy as np

D_MODEL = 2048
BATCH = 2
SEQ = 4096
DEPTH = 2

N_MIXERS = 2
EPS = 1e-6
D_FF = 5632
GM_WIDTH = D_MODEL
CHUNK = 128
GM_GROUPS = 16
GM_GROUP_DIM = GM_WIDTH // GM_GROUPS
CONV_WIDTH = D_MODEL
CONV_K = 31
N_SUB = 3
N_MOD = 3
N_A = (DEPTH + 1) // 2
N_B = DEPTH // 2

kernel_name = "macaron_gmlp_conformer_hybrid"


def rms_norm(x, g):
    xf = x.astype(jnp.float32)
    y = xf * lax.rsqrt(jnp.mean(xf * xf, axis=-1, keepdims=True) + EPS)
    return (y * g.astype(jnp.float32)).astype(x.dtype)


def layer_norm(x, g, b):
    xf = x.astype(jnp.float32)
    mu = jnp.mean(xf, axis=-1, keepdims=True)
    var = jnp.mean(jnp.square(xf - mu), axis=-1, keepdims=True)
    y = (xf - mu) * lax.rsqrt(var + EPS)
    return (y * g.astype(jnp.float32) + b.astype(jnp.float32)).astype(x.dtype)


def modulate(h, shift, scale):
    return h * (1 + scale[:, None, :]) + shift[:, None, :]


def swiglu_ffn(h, w_in, w_out):
    gate, up = jnp.split(h @ w_in, 2, axis=-1)
    return (jax.nn.silu(gate) * up) @ w_out


def gmlp_mixer(h, w_in, ln_g, ln_b, ws, bs, w_out):
    b, t, _ = h.shape
    z = jax.nn.gelu(h @ w_in, approximate=False)
    u, v = jnp.split(z, 2, axis=-1)
    v = layer_norm(v, ln_g, ln_b)
    v = v.reshape(b, t // CHUNK, CHUNK, GM_GROUPS, GM_GROUP_DIM)
    causal = jnp.tril(jnp.ones((CHUNK, CHUNK), dtype=bool))
    ws_c = jnp.where(causal[None], ws, jnp.zeros_like(ws))
    v = jnp.einsum("hts,bcshd->bcthd", ws_c, v) + bs.T[None, None, :, :, None]
    s = u * v.reshape(b, t, GM_WIDTH)
    return s @ w_out


def conv_mixer(h, w_in, b_in, dw_w, dw_b, ln_g, ln_b, w_out, b_out):
    a, g = jnp.split(h @ w_in + b_in, 2, axis=-1)
    y = a * jax.nn.sigmoid(g)
    y = lax.conv_general_dilated(
        y, dw_w[:, None, :].astype(y.dtype),
        window_strides=(1,), padding=[(CONV_K - 1, 0)],
        dimension_numbers=("NWC", "WIO", "NWC"),
        feature_group_count=CONV_WIDTH) + dw_b
    y = jax.nn.silu(layer_norm(y, ln_g, ln_b))
    return y @ w_out + b_out


def setup_inputs(seed: int = 0) -> dict:
    key = jax.random.key(seed)
    ks = jax.random.split(key, 24)
    D, F, E, C, H, L = D_MODEL, D_FF, GM_WIDTH, CONV_WIDTH, GM_GROUPS, CHUNK
    nrm = lambda k, shape, s: (jax.random.normal(k, shape, jnp.float32) * s).astype(jnp.float32)

    x = nrm(ks[0], (BATCH, SEQ, D), 1.0)
    c = nrm(ks[1], (BATCH, D), 1.0)
    ada_w = nrm(ks[2], (DEPTH, D, N_SUB * N_MOD * D), 0.1 * D ** -0.5)
    ada_b = nrm(ks[3], (DEPTH, N_SUB, N_MOD, D), 0.02).at[:, :, 2].add(1.0).reshape(DEPTH, N_SUB * N_MOD * D)
    norm_g = 1.0 + nrm(ks[4], (DEPTH, N_SUB, D), 0.02)
    ffn_w_in = nrm(ks[5], (DEPTH, 2, D, 2 * F), D ** -0.5)
    ffn_w_out = nrm(ks[6], (DEPTH, 2, F, D), F ** -0.5)

    gm_w_in = nrm(ks[7], (N_A, D, 2 * E), D ** -0.5)
    gm_ln_g = 1.0 + nrm(ks[8], (N_A, E), 0.02)
    gm_ln_b = nrm(ks[9], (N_A, E), 0.02)
    gm_ws = nrm(ks[10], (N_A, H, L, L), L ** -0.5)
    gm_bs = 1.0 + nrm(ks[11], (N_A, H, L), 0.02)
    gm_w_out = nrm(ks[12], (N_A, E, D), E ** -0.5)

    cv_w_in = nrm(ks[13], (N_B, D, 2 * C), D ** -0.5)
    cv_b_in = nrm(ks[14], (N_B, 2 * C), 0.02)
    cv_dw_w = nrm(ks[15], (N_B, CONV_K, C), CONV_K ** -0.5)
    cv_dw_b = nrm(ks[16], (N_B, C), 0.02)
    cv_ln_g = 1.0 + nrm(ks[17], (N_B, C), 0.02)
    cv_ln_b = nrm(ks[18], (N_B, C), 0.02)
    cv_w_out = nrm(ks[19], (N_B, C, D), C ** -0.5)
    cv_b_out = nrm(ks[20], (N_B, D), 0.02)

    final_g = 1.0 + nrm(ks[21], (D,), 0.02)
    return {"x": x, "c": c, "ada_w": ada_w, "ada_b": ada_b, "norm_g": norm_g,
            "ffn_w_in": ffn_w_in, "ffn_w_out": ffn_w_out,
            "gm_w_in": gm_w_in, "gm_ln_g": gm_ln_g, "gm_ln_b": gm_ln_b,
            "gm_ws": gm_ws, "gm_bs": gm_bs, "gm_w_out": gm_w_out,
            "cv_w_in": cv_w_in, "cv_b_in": cv_b_in, "cv_dw_w": cv_dw_w, "cv_dw_b": cv_dw_b,
            "cv_ln_g": cv_ln_g, "cv_ln_b": cv_ln_b, "cv_w_out": cv_w_out, "cv_b_out": cv_b_out,
            "final_g": final_g}


def reference(x, c, ada_w, ada_b, norm_g, ffn_w_in, ffn_w_out,
              gm_w_in, gm_ln_g, gm_ln_b, gm_ws, gm_bs, gm_w_out,
              cv_w_in, cv_b_in, cv_dw_w, cv_dw_b, cv_ln_g, cv_ln_b, cv_w_out, cv_b_out,
              final_g):
    bsz = x.shape[0]
    cond = jax.nn.silu(c)
    for i in range(DEPTH):
        mod = (cond @ ada_w[i] + ada_b[i]).reshape(bsz, N_SUB, N_MOD, D_MODEL)
        shift, scale, gate = mod[:, :, 0], mod[:, :, 1], mod[:, :, 2]

        h = modulate(rms_norm(x, norm_g[i, 0]), shift[:, 0], scale[:, 0])
        x = x + 0.5 * gate[:, 0, None, :] * swiglu_ffn(h, ffn_w_in[i, 0], ffn_w_out[i, 0])

        h = modulate(rms_norm(x, norm_g[i, 1]), shift[:, 1], scale[:, 1])
        j = i // N_MIXERS
        if i % N_MIXERS == 0:
            y = gmlp_mixer(h, gm_w_in[j], gm_ln_g[j], gm_ln_b[j], gm_ws[j], gm_bs[j], gm_w_out[j])
        else:
            y = conv_mixer(h, cv_w_in[j], cv_b_in[j], cv_dw_w[j], cv_dw_b[j],
                           cv_ln_g[j], cv_ln_b[j], cv_w_out[j], cv_b_out[j])
        x = x + gate[:, 1, None, :] * y

        h = modulate(rms_norm(x, norm_g[i, 2]), shift[:, 2], scale[:, 2])
        x = x + 0.5 * gate[:, 2, None, :] * swiglu_ffn(h, ffn_w_in[i, 1], ffn_w_out[i, 1])
    return rms_norm(x, final_g)
```

```python
import functools

import jax
import jax.numpy as jnp
from jax import lax
from jax.experimental import pallas as pl
from jax.experimental.pallas import tpu as pltpu

EPS = 1e-6
N_MOD = 3
CHUNK = 128
MOD_ROWS = 8
CONV_HALO = 32

V7X_VMEM_BYTES = 64 * 1024 * 1024
V7X_LANES = 128
SUBLANES = 8

F32 = jnp.float32
BF16 = jnp.bfloat16


def _vmem_limit(pipelined, single, scratch, temps):
    need = 2 * pipelined + single + scratch + temps
    assert need <= V7X_VMEM_BYTES, need
    return int(need)


def _nbytes(shape, dtype):
    n = 1
    for s in shape:
        n *= s
    return n * jnp.dtype(dtype).itemsize


def _dot(a, b):
    return jnp.dot(a, b, preferred_element_type=F32)


def _norm_mod_rows(x, g, shift, scale):
    ms = jnp.mean(x * x, axis=-1, keepdims=True)
    y = x * lax.rsqrt(ms + EPS) * g
    return y * (1.0 + scale) + shift


def _fill_h(h_ref, x_ref, g_ref, shift_ref, scale_ref, b, rows):
    tm = x_ref.shape[0]
    g = g_ref[...]
    shift = shift_ref[pl.ds(b, 1), :]
    scale = scale_ref[pl.ds(b, 1), :]

    def body(r, carry):
        rs = pl.ds(pl.multiple_of(r * rows, rows), rows)
        h_ref[rs, :] = _norm_mod_rows(x_ref[rs, :], g, shift, scale).astype(BF16)
        return carry

    lax.fori_loop(0, tm // rows, body, 0)


def _ada_kernel(c_ref, w_ref, b_ref, o_ref):
    c = c_ref[...]
    cond = (c * jax.nn.sigmoid(c)).astype(BF16)
    o_ref[...] = _dot(cond, w_ref[...].astype(BF16)) + b_ref[...]


def _ada_table(c, ada_w, ada_b, tn=1024):
    depth, d, n = ada_w.shape
    bsz = c.shape[0]
    assert bsz <= MOD_ROWS and n % d == 0 and d % tn == 0
    per = d // tn
    c8 = jnp.pad(c, ((0, MOD_ROWS - bsz), (0, 0)))
    b4 = ada_b.reshape(depth, n // d, 1, d)
    return pl.pallas_call(
        _ada_kernel,
        grid=(depth, n // tn),
        in_specs=[
            pl.BlockSpec((MOD_ROWS, d), lambda l, j: (0, 0)),
            pl.BlockSpec((None, d, tn), lambda l, j: (l, 0, j)),
            pl.BlockSpec((None, None, 1, tn), lambda l, j: (l, j // per, 0, j % per)),
        ],
        out_specs=pl.BlockSpec((None, None, MOD_ROWS, tn), lambda l, j: (l, j // per, 0, j % per)),
        out_shape=jax.ShapeDtypeStruct((depth, n // d, MOD_ROWS, d), F32),
        compiler_params=pltpu.CompilerParams(
            dimension_semantics=("parallel", "parallel"),
            vmem_limit_bytes=_vmem_limit(
                _nbytes((d, tn), F32) + _nbytes((MOD_ROWS, d + 2 * tn), F32), 0, 0,
                2 * _nbytes((d, tn), BF16) + 4 * _nbytes((MOD_ROWS, d), F32))),
        name="ada",
    )(c8, ada_w, b4)


def _mod_spec(layer, row, d):
    return pl.BlockSpec((None, None, MOD_ROWS, d), lambda *_: (layer, row, 0, 0))


def _ffn_kernel(x_ref, g_ref, shift_ref, scale_ref, gate_ref, wg_ref, wu_ref, wo_ref, *rest,
                blocks_per_batch, n_f, n_chunk, rows, final):
    if final:
        fg_ref, o_ref, h_ref = rest
    else:
        o_ref, h_ref = rest
    i = pl.program_id(0)
    f = pl.program_id(1)
    b = i // blocks_per_batch
    tm, d = o_ref.shape

    @pl.when(f == 0)
    def _():
        _fill_h(h_ref, x_ref, g_ref, shift_ref, scale_ref, b, rows)
        o_ref[...] = x_ref[...]

    half_gate = 0.5 * gate_ref[pl.ds(b, 1), :]
    h = h_ref[...]
    gt = _dot(h, wg_ref[...])
    up = _dot(h, wu_ref[...])
    a = (gt * jax.nn.sigmoid(gt) * up).astype(BF16)
    for n0 in range(0, d, n_chunk):
        ns = slice(n0, n0 + n_chunk)
        o_ref[:, ns] += half_gate[:, ns] * _dot(a, wo_ref[:, ns])

    if final:
        @pl.when(f == n_f - 1)
        def _():
            fg = fg_ref[...]

            def body(r, carry):
                rs = pl.ds(pl.multiple_of(r * rows, rows), rows)
                o = o_ref[rs, :]
                ms = jnp.mean(o * o, axis=-1, keepdims=True)
                o_ref[rs, :] = o * lax.rsqrt(ms + EPS) * fg
                return carry

            lax.fori_loop(0, tm // rows, body, 0)


def _ffn(x2d, mods, norm_g3, w_in, w_out, layer, sub, which, seq, final_g=None, tm=1024, tf=512,
         n_chunk=512, rows=128):
    m, d = x2d.shape
    f_dim = w_out.shape[2]
    assert seq % tm == 0 and f_dim % tf == 0 and d % n_chunk == 0 and tm % rows == 0
    n_f = f_dim // tf
    row0 = sub * N_MOD
    in_specs = [
        pl.BlockSpec((tm, d), lambda i, f: (i, 0), pipeline_mode=pl.Buffered(1)),
        pl.BlockSpec((None, 1, d), lambda i, f: (layer * 3 + sub, 0, 0)),
        _mod_spec(layer, row0 + 0, d),
        _mod_spec(layer, row0 + 1, d),
        _mod_spec(layer, row0 + 2, d),
        pl.BlockSpec((None, None, d, tf), lambda i, f: (layer, which, 0, f)),
        pl.BlockSpec((None, None, d, tf), lambda i, f: (layer, which, 0, f + n_f)),
        pl.BlockSpec((None, None, tf, d), lambda i, f: (layer, which, f, 0)),
    ]
    args = [x2d, norm_g3, mods, mods, mods, w_in, w_in, w_out]
    if final_g is not None:
        in_specs.append(pl.BlockSpec((1, d), lambda i, f: (0, 0)))
        args.append(final_g.reshape(1, d))
    kern = functools.partial(_ffn_kernel, blocks_per_batch=seq // tm, n_f=n_f, n_chunk=n_chunk,
                             rows=rows, final=final_g is not None)
    vmem = _vmem_limit(
        pipelined=_nbytes((tm, d), F32) + 3 * _nbytes((d, tf), BF16) + 5 * _nbytes((MOD_ROWS, d), F32),
        single=_nbytes((tm, d), F32),
        scratch=_nbytes((tm, d), BF16),
        temps=5 * _nbytes((tm, tf), F32) + _nbytes((tm, tf), BF16) + 3 * _nbytes((tm, n_chunk), F32))
    return pl.pallas_call(
        kern,
        grid=(m // tm, n_f),
        in_specs=in_specs,
        out_specs=pl.BlockSpec((tm, d), lambda i, f: (i, 0)),
        out_shape=jax.ShapeDtypeStruct((m, d), F32),
        scratch_shapes=[pltpu.VMEM((tm, d), BF16)],
        compiler_params=pltpu.CompilerParams(
            dimension_semantics=("parallel", "arbitrary"), vmem_limit_bytes=vmem),
        name="ffn",
    )(*args)


def _gmlp_proj_kernel(x_ref, g_ref, shift_ref, scale_ref, w_ref, z_ref, h_ref, *, blocks_per_batch, rows):
    i = pl.program_id(0)

    @pl.when(pl.program_id(1) == 0)
    def _():
        _fill_h(h_ref, x_ref, g_ref, shift_ref, scale_ref, i // blocks_per_batch, rows)

    z = _dot(h_ref[...], w_ref[...])
    z_ref[...] = 0.5 * z * (1.0 + lax.erf(z * (2.0 ** -0.5)))


def _gmlp_proj(x2d, mods, norm_g3, w_in, layer, jm, seq, tm=1024, tn=512, rows=128):
    m, d = x2d.shape
    n = w_in.shape[2]
    assert seq % tm == 0 and n % tn == 0
    kern = functools.partial(_gmlp_proj_kernel, blocks_per_batch=seq // tm, rows=rows)
    vmem = _vmem_limit(
        pipelined=_nbytes((tm, d), F32) + _nbytes((d, tn), BF16) + _nbytes((tm, tn), F32)
        + 3 * _nbytes((MOD_ROWS, d), F32),
        single=0, scratch=_nbytes((tm, d), BF16), temps=5 * _nbytes((tm, tn), F32))
    return pl.pallas_call(
        kern,
        grid=(m // tm, n // tn),
        in_specs=[
            pl.BlockSpec((tm, d), lambda i, j: (i, 0)),
            pl.BlockSpec((None, 1, d), lambda i, j: (layer * 3 + 1, 0, 0)),
            _mod_spec(layer, N_MOD + 0, d),
            _mod_spec(layer, N_MOD + 1, d),
            pl.BlockSpec((None, d, tn), lambda i, j: (jm, 0, j)),
        ],
        out_specs=pl.BlockSpec((tm, tn), lambda i, j: (i, j)),
        out_shape=jax.ShapeDtypeStruct((m, n), F32),
        scratch_shapes=[pltpu.VMEM((tm, d), BF16)],
        compiler_params=pltpu.CompilerParams(
            dimension_semantics=("parallel", "arbitrary"), vmem_limit_bytes=vmem),
        name="gmlp_proj",
    )(x2d, norm_g3, mods, mods, w_in)


def _layer_norm_rows(v, g, b):
    mu = jnp.mean(v, axis=-1, keepdims=True)
    dv = v - mu
    var = jnp.mean(dv * dv, axis=-1, keepdims=True)
    return dv * lax.rsqrt(var + EPS) * g + b


def _gmlp_mix_kernel(x_ref, u_ref, v_ref, lng_ref, lnb_ref, ws_ref, bs_ref, gate_ref, wo_ref, o_ref, s_ref,
                     *, blocks_per_batch, n_chunk):
    b = pl.program_id(0) // blocks_per_batch
    tm, e = s_ref.shape
    n_heads, length, _ = ws_ref.shape
    dh = e // n_heads
    lng = lng_ref[...]
    lnb = lnb_ref[...]
    causal = (lax.broadcasted_iota(jnp.int32, (length, length), 0)
              >= lax.broadcasted_iota(jnp.int32, (length, length), 1))

    def chunk(c, carry):
        rs = pl.ds(pl.multiple_of(c * length, length), length)
        vn = _layer_norm_rows(v_ref[rs, :], lng, lnb).astype(BF16)
        for hd in range(n_heads):
            cs = slice(hd * dh, (hd + 1) * dh)
            w_h = jnp.where(causal, ws_ref[hd], 0.0).astype(BF16)
            mixed = _dot(w_h, vn[:, cs]) + bs_ref[hd]
            s_ref[rs, cs] = (u_ref[rs, cs] * mixed).astype(BF16)
        return carry

    lax.fori_loop(0, tm // length, chunk, 0)
    gate = gate_ref[pl.ds(b, 1), :]
    s = s_ref[...]
    for n0 in range(0, o_ref.shape[1], n_chunk):
        ns = slice(n0, n0 + n_chunk)
        o_ref[:, ns] = x_ref[:, ns] + gate[:, ns] * _dot(s, wo_ref[:, ns])


def _gmlp_mix(x2d, z, mods, ln_g, ln_b, ws, bs, w_out, layer, jm, seq, tm=512, n_chunk=512):
    m, d = x2d.shape
    e = w_out.shape[1]
    n_heads, length = ws.shape[1], ws.shape[2]
    assert length == CHUNK and seq % tm == 0 and tm % length == 0 and (e // n_heads) % V7X_LANES == 0
    assert d % n_chunk == 0
    kern = functools.partial(_gmlp_mix_kernel, blocks_per_batch=seq // tm, n_chunk=n_chunk)
    vmem = _vmem_limit(
        pipelined=_nbytes((tm, d), F32) * 2 + 2 * _nbytes((tm, e), F32) + 2 * _nbytes((n_heads, length, length), F32)
        + 3 * _nbytes((MOD_ROWS, d), F32),
        single=_nbytes((e, d), BF16), scratch=_nbytes((tm, e), BF16),
        temps=3 * _nbytes((tm, n_chunk), F32) + 6 * _nbytes((length, e), F32))
    return pl.pallas_call(
        kern,
        grid=(m // tm,),
        in_specs=[
            pl.BlockSpec((tm, d), lambda i: (i, 0)),
            pl.BlockSpec((tm, e), lambda i: (i, 0)),
            pl.BlockSpec((tm, e), lambda i: (i, 1)),
            pl.BlockSpec((None, 1, e), lambda i: (jm, 0, 0)),
            pl.BlockSpec((None, 1, e), lambda i: (jm, 0, 0)),
            pl.BlockSpec((None, n_heads, length, length), lambda i: (jm, 0, 0, 0)),
            pl.BlockSpec((None, n_heads, length, 1), lambda i: (jm, 0, 0, 0)),
            _mod_spec(layer, N_MOD + 2, d),
            pl.BlockSpec((None, e, d), lambda i: (jm, 0, 0), pipeline_mode=pl.Buffered(1)),
        ],
        out_specs=pl.BlockSpec((tm, d), lambda i: (i, 0)),
        out_shape=jax.ShapeDtypeStruct((m, d), F32),
        scratch_shapes=[pltpu.VMEM((tm, e), BF16)],
        compiler_params=pltpu.CompilerParams(dimension_semantics=("parallel",), vmem_limit_bytes=vmem),
        name="gmlp_mix",
    )(x2d, z, z, ln_g[:, None, :], ln_b[:, None, :], ws, bs[..., None], mods, w_out)


def _conv_proj_kernel(x_ref, g_ref, shift_ref, scale_ref, wa_ref, wg_ref, ba_ref, bg_ref, y_ref, h_ref,
                      *, blocks_per_batch, rows):
    i = pl.program_id(0)

    @pl.when(pl.program_id(1) == 0)
    def _():
        _fill_h(h_ref, x_ref, g_ref, shift_ref, scale_ref, i // blocks_per_batch, rows)

    h = h_ref[...]
    a = _dot(h, wa_ref[...]) + ba_ref[...]
    gl = _dot(h, wg_ref[...]) + bg_ref[...]
    y_ref[...] = a * jax.nn.sigmoid(gl)


def _conv_proj(x2d, mods, norm_g3, w_in, b_in, layer, jm, seq, tm=1024, tn=512, rows=128):
    m, d = x2d.shape
    c = w_in.shape[2] // 2
    assert seq % tm == 0 and c % tn == 0
    n_j = c // tn
    kern = functools.partial(_conv_proj_kernel, blocks_per_batch=seq // tm, rows=rows)
    vmem = _vmem_limit(
        pipelined=_nbytes((tm, d), F32) + 2 * _nbytes((d, tn), BF16) + _nbytes((tm, tn), F32)
        + 3 * _nbytes((MOD_ROWS, d), F32) + 2 * _nbytes((MOD_ROWS, tn), F32),
        single=0, scratch=_nbytes((tm, d), BF16), temps=6 * _nbytes((tm, tn), F32))
    return pl.pallas_call(
        kern,
        grid=(m // tm, n_j),
        in_specs=[
            pl.BlockSpec((tm, d), lambda i, j: (i, 0)),
            pl.BlockSpec((None, 1, d), lambda i, j: (layer * 3 + 1, 0, 0)),
            _mod_spec(layer, N_MOD + 0, d),
            _mod_spec(layer, N_MOD + 1, d),
            pl.BlockSpec((None, d, tn), lambda i, j: (jm, 0, j)),
            pl.BlockSpec((None, d, tn), lambda i, j: (jm, 0, j + n_j)),
            pl.BlockSpec((None, 1, tn), lambda i, j: (jm, 0, j)),
            pl.BlockSpec((None, 1, tn), lambda i, j: (jm, 0, j + n_j)),
        ],
        out_specs=pl.BlockSpec((tm, tn), lambda i, j: (i, j)),
        out_shape=jax.ShapeDtypeStruct((m, c), F32),
        scratch_shapes=[pltpu.VMEM((tm, d), BF16)],
        compiler_params=pltpu.CompilerParams(
            dimension_semantics=("parallel", "arbitrary"), vmem_limit_bytes=vmem),
        name="conv_proj",
    )(x2d, norm_g3, mods, mods, w_in, w_in, b_in[:, None, :], b_in[:, None, :])


def _conv_mix_kernel(x_ref, y_ref, yprev_ref, dww_ref, dwb_ref, lng_ref, lnb_ref, gate_ref, wo_ref, bo_ref,
                     o_ref, win_ref, yc_ref, s_ref, *, blocks_per_batch, rows, n_chunk):
    i = pl.program_id(0)
    b = i // blocks_per_batch
    tm, c = y_ref.shape
    k_taps = dww_ref.shape[0]
    first = (i % blocks_per_batch) == 0

    win_ref[0:CONV_HALO, :] = jnp.where(first, 0.0, yprev_ref[...])
    win_ref[CONV_HALO:, :] = y_ref[...]

    lng = lng_ref[...]
    lnb = lnb_ref[...]
    base = CONV_HALO - (k_taps - 1)

    def chunk(r, carry):
        r0 = pl.multiple_of(r * rows, rows)
        for s in range(c // V7X_LANES):
            cs = slice(s * V7X_LANES, (s + 1) * V7X_LANES)
            acc = jnp.broadcast_to(dwb_ref[:, cs], (rows, V7X_LANES))
            for p in range(SUBLANES):
                part = None
                for a in range(CONV_HALO // SUBLANES + 1):
                    k = a * SUBLANES + p - base
                    if 0 <= k < k_taps:
                        n_load = rows + (SUBLANES if p else 0)
                        term = dww_ref[k:k + 1, cs] * win_ref[pl.ds(r0 + a * SUBLANES, n_load), cs]
                        part = term if part is None else part + term
                acc = acc + part[p:p + rows]
            yc_ref[pl.ds(r0, rows), cs] = acc
        yl = _layer_norm_rows(yc_ref[pl.ds(r0, rows), :], lng, lnb)
        s_ref[pl.ds(r0, rows), :] = (yl * jax.nn.sigmoid(yl)).astype(BF16)
        return carry

    lax.fori_loop(0, tm // rows, chunk, 0)
    gate = gate_ref[pl.ds(b, 1), :]
    s = s_ref[...]
    for n0 in range(0, o_ref.shape[1], n_chunk):
        ns = slice(n0, n0 + n_chunk)
        o_ref[:, ns] = x_ref[:, ns] + gate[:, ns] * (_dot(s, wo_ref[:, ns]) + bo_ref[:, ns])


def _conv_mix(x2d, y, mods, dw_w, dw_b, ln_g, ln_b, w_out, b_out, layer, jm, seq, tm=512, rows=64,
              n_chunk=512):
    m, d = x2d.shape
    c = y.shape[1]
    k_taps = dw_w.shape[1]
    assert k_taps - 1 <= CONV_HALO and seq % tm == 0 and tm % CONV_HALO == 0 and tm % rows == 0
    assert d % n_chunk == 0
    per = tm // CONV_HALO
    kern = functools.partial(_conv_mix_kernel, blocks_per_batch=seq // tm, rows=rows, n_chunk=n_chunk)
    vmem = _vmem_limit(
        pipelined=2 * _nbytes((tm, d), F32) + _nbytes((tm, c), F32) + _nbytes((CONV_HALO, c), F32)
        + _nbytes((CONV_HALO, c), F32) + 6 * _nbytes((MOD_ROWS, d), F32),
        single=_nbytes((c, d), BF16),
        scratch=_nbytes((tm + CONV_HALO, c), F32) + _nbytes((tm, c), F32) + _nbytes((tm, c), BF16),
        temps=4 * _nbytes((tm, n_chunk), F32) + 8 * _nbytes((rows, c), F32))
    return pl.pallas_call(
        kern,
        grid=(m // tm,),
        in_specs=[
            pl.BlockSpec((tm, d), lambda i: (i, 0)),
            pl.BlockSpec((tm, c), lambda i: (i, 0)),
            pl.BlockSpec((CONV_HALO, c), lambda i: (jnp.maximum(i * per - 1, 0), 0)),
            pl.BlockSpec((None, k_taps, c), lambda i: (jm, 0, 0)),
            pl.BlockSpec((None, 1, c), lambda i: (jm, 0, 0)),
            pl.BlockSpec((None, 1, c), lambda i: (jm, 0, 0)),
            pl.BlockSpec((None, 1, c), lambda i: (jm, 0, 0)),
            _mod_spec(layer, N_MOD + 2, d),
            pl.BlockSpec((None, c, d), lambda i: (jm, 0, 0), pipeline_mode=pl.Buffered(1)),
            pl.BlockSpec((None, 1, d), lambda i: (jm, 0, 0)),
        ],
        out_specs=pl.BlockSpec((tm, d), lambda i: (i, 0)),
        out_shape=jax.ShapeDtypeStruct((m, d), F32),
        scratch_shapes=[pltpu.VMEM((tm + CONV_HALO, c), F32), pltpu.VMEM((tm, c), F32), pltpu.VMEM((tm, c), BF16)],
        compiler_params=pltpu.CompilerParams(dimension_semantics=("parallel",), vmem_limit_bytes=vmem),
        name="conv_mix",
    )(x2d, y, y, dw_w, dw_b[:, None, :], ln_g[:, None, :], ln_b[:, None, :], mods, w_out, b_out[:, None, :])


def kernel(x, c, ada_w, ada_b, norm_g, ffn_w_in, ffn_w_out, gm_w_in, gm_ln_g, gm_ln_b, gm_ws, gm_bs, gm_w_out,
           cv_w_in, cv_b_in, cv_dw_w, cv_dw_b, cv_ln_g, cv_ln_b, cv_w_out, cv_b_out, final_g):
    bsz, seq, d = x.shape
    depth, n_sub = norm_g.shape[0], norm_g.shape[1]
    n_mixers = 2
    assert n_sub == 3 and ada_w.shape[2] == n_sub * N_MOD * d

    mods = _ada_table(c, ada_w, ada_b)
    norm_g3 = norm_g.reshape(depth * n_sub, 1, d)
    ffn_w_in, ffn_w_out = ffn_w_in.astype(BF16), ffn_w_out.astype(BF16)
    gm_w_in, gm_w_out = gm_w_in.astype(BF16), gm_w_out.astype(BF16)
    cv_w_in, cv_w_out = cv_w_in.astype(BF16), cv_w_out.astype(BF16)

    h = x.reshape(bsz * seq, d)
    for i in range(depth):
        h = _ffn(h, mods, norm_g3, ffn_w_in, ffn_w_out, i, 0, 0, seq)
        jm = i // n_mixers
        if i % n_mixers == 0:
            z = _gmlp_proj(h, mods, norm_g3, gm_w_in, i, jm, seq)
            h = _gmlp_mix(h, z, mods, gm_ln_g, gm_ln_b, gm_ws, gm_bs, gm_w_out, i, jm, seq)
        else:
            y = _conv_proj(h, mods, norm_g3, cv_w_in, cv_b_in, i, jm, seq)
            h = _conv_mix(h, y, mods, cv_dw_w, cv_dw_b, cv_ln_g, cv_ln_b, cv_w_out, cv_b_out, i, jm, seq)
        h = _ffn(h, mods, norm_g3, ffn_w_in, ffn_w_out, i, 2, 1, seq,
                 final_g=final_g if i == depth - 1 else None)
    return h.reshape(bsz, seq, d)
```

```python
import functools

import jax
import jax.numpy as jnp
from jax import lax
from jax.experimental import pallas as pl
from jax.experimental.pallas import tpu as pltpu

EPS = 1e-6
N_MOD = 3
CHUNK = 128
MOD_ROWS = 8
CONV_HALO = 32

V7X_VMEM_BYTES = 64 * 1024 * 1024
V7X_LANES = 128
SUBLANES = 8

F32 = jnp.float32
BF16 = jnp.bfloat16


def _vmem_limit(pipelined, single, scratch, temps):
    need = 2 * pipelined + single + scratch + temps
    cap = V7X_VMEM_BYTES - 1024 * 1024
    assert need <= cap, need
    return int(min(need + need // 10, cap))


def _nbytes(shape, dtype):
    n = 1
    for s in shape:
        n *= s
    return n * jnp.dtype(dtype).itemsize


def _dot(a, b):
    return jnp.dot(a, b, preferred_element_type=F32)


def _norm_mod_rows(x, g, shift, scale):
    ms = jnp.mean(x * x, axis=-1, keepdims=True)
    y = x * lax.rsqrt(ms + EPS) * g
    return y * (1.0 + scale) + shift


def _fill_h(h_ref, x_ref, g_ref, shift_ref, scale_ref, b, rows):
    tm = x_ref.shape[0]
    g = g_ref[...]
    shift = shift_ref[pl.ds(b, 1), :]
    scale = scale_ref[pl.ds(b, 1), :]

    def body(r, carry):
        rs = pl.ds(pl.multiple_of(r * rows, rows), rows)
        h_ref[rs, :] = _norm_mod_rows(x_ref[rs, :], g, shift, scale).astype(BF16)
        return carry

    lax.fori_loop(0, tm // rows, body, 0)


def _ada_kernel(c_ref, w_ref, b_ref, o_ref):
    c = c_ref[...]
    cond = (c * jax.nn.sigmoid(c)).astype(BF16)
    o_ref[...] = _dot(cond, w_ref[...].astype(BF16)) + b_ref[...]


def _ada_table(c, ada_w, ada_b, tn=1024):
    depth, d, n = ada_w.shape
    bsz = c.shape[0]
    assert bsz <= MOD_ROWS and n % d == 0 and d % tn == 0
    per = d // tn
    c8 = jnp.pad(c, ((0, MOD_ROWS - bsz), (0, 0)))
    b4 = ada_b.reshape(depth, n // d, 1, d)
    return pl.pallas_call(
        _ada_kernel,
        grid=(depth, n // tn),
        in_specs=[
            pl.BlockSpec((MOD_ROWS, d), lambda l, j: (0, 0)),
            pl.BlockSpec((None, d, tn), lambda l, j: (l, 0, j)),
            pl.BlockSpec((None, None, 1, tn), lambda l, j: (l, j // per, 0, j % per)),
        ],
        out_specs=pl.BlockSpec((None, None, MOD_ROWS, tn), lambda l, j: (l, j // per, 0, j % per)),
        out_shape=jax.ShapeDtypeStruct((depth, n // d, MOD_ROWS, d), F32),
        compiler_params=pltpu.CompilerParams(
            dimension_semantics=("parallel", "parallel"),
            vmem_limit_bytes=_vmem_limit(
                _nbytes((d, tn), F32) + _nbytes((MOD_ROWS, d + 2 * tn), F32), 0, 0,
                2 * _nbytes((d, tn), BF16) + 4 * _nbytes((MOD_ROWS, d), F32))),
        name="ada",
    )(c8, ada_w, b4)


def _mod_spec(layer, row, d):
    return pl.BlockSpec((None, None, MOD_ROWS, d), lambda *_: (layer, row, 0, 0))


def _ffn_kernel(x_ref, g_ref, shift_ref, scale_ref, gate_ref, wg_ref, wu_ref, wo_ref, *rest,
                blocks_per_batch, n_f, n_chunk, rows, final):
    if final:
        fg_ref, o_ref, h_ref = rest
    else:
        o_ref, h_ref = rest
    i = pl.program_id(0)
    f = pl.program_id(1)
    b = i // blocks_per_batch
    tm, d = o_ref.shape

    @pl.when(f == 0)
    def _():
        _fill_h(h_ref, x_ref, g_ref, shift_ref, scale_ref, b, rows)
        o_ref[...] = x_ref[...]

    half_gate = 0.5 * gate_ref[pl.ds(b, 1), :]
    h = h_ref[...]
    gt = _dot(h, wg_ref[...].astype(BF16))
    up = _dot(h, wu_ref[...].astype(BF16))
    a = (gt * jax.nn.sigmoid(gt) * up).astype(BF16)
    for n0 in range(0, d, n_chunk):
        ns = slice(n0, n0 + n_chunk)
        o_ref[:, ns] += half_gate[:, ns] * _dot(a, wo_ref[:, ns].astype(BF16))

    if final:
        @pl.when(f == n_f - 1)
        def _():
            fg = fg_ref[...]

            def body(r, carry):
                rs = pl.ds(pl.multiple_of(r * rows, rows), rows)
                o = o_ref[rs, :]
                ms = jnp.mean(o * o, axis=-1, keepdims=True)
                o_ref[rs, :] = o * lax.rsqrt(ms + EPS) * fg
                return carry

            lax.fori_loop(0, tm // rows, body, 0)


def _ffn(x2d, mods, norm_g3, w_in, w_out, layer, sub, which, seq, final_g=None, tm=1024, tf=512,
         n_chunk=512, rows=128):
    m, d = x2d.shape
    f_dim = w_out.shape[2]
    assert seq % tm == 0 and f_dim % tf == 0 and d % n_chunk == 0 and tm % rows == 0
    n_f = f_dim // tf
    row0 = sub * N_MOD
    in_specs = [
        pl.BlockSpec((tm, d), lambda i, f: (i, 0), pipeline_mode=pl.Buffered(1)),
        pl.BlockSpec((None, 1, d), lambda i, f: (layer * 3 + sub, 0, 0)),
        _mod_spec(layer, row0 + 0, d),
        _mod_spec(layer, row0 + 1, d),
        _mod_spec(layer, row0 + 2, d),
        pl.BlockSpec((None, None, d, tf), lambda i, f: (layer, which, 0, f)),
        pl.BlockSpec((None, None, d, tf), lambda i, f: (layer, which, 0, f + n_f)),
        pl.BlockSpec((None, None, tf, d), lambda i, f: (layer, which, f, 0)),
    ]
    args = [x2d, norm_g3, mods, mods, mods, w_in, w_in, w_out]
    if final_g is not None:
        in_specs.append(pl.BlockSpec((1, d), lambda i, f: (0, 0)))
        args.append(final_g.reshape(1, d))
    kern = functools.partial(_ffn_kernel, blocks_per_batch=seq // tm, n_f=n_f, n_chunk=n_chunk,
                             rows=rows, final=final_g is not None)
    vmem = _vmem_limit(
        pipelined=_nbytes((tm, d), F32) + 3 * _nbytes((d, tf), w_in.dtype) + 5 * _nbytes((MOD_ROWS, d), F32),
        single=_nbytes((tm, d), F32),
        scratch=_nbytes((tm, d), BF16),
        temps=2 * _nbytes((tm, tf), F32) + _nbytes((tm, tf), BF16))
    return pl.pallas_call(
        kern,
        grid=(m // tm, n_f),
        in_specs=in_specs,
        out_specs=pl.BlockSpec((tm, d), lambda i, f: (i, 0)),
        out_shape=jax.ShapeDtypeStruct((m, d), F32),
        scratch_shapes=[pltpu.VMEM((tm, d), BF16)],
        compiler_params=pltpu.CompilerParams(
            dimension_semantics=("parallel", "arbitrary"), vmem_limit_bytes=vmem),
        name="ffn",
    )(*args)


def _gmlp_proj_kernel(x_ref, g_ref, shift_ref, scale_ref, w_ref, z_ref, h_ref, *, blocks_per_batch, rows):
    i = pl.program_id(0)

    @pl.when(pl.program_id(1) == 0)
    def _():
        _fill_h(h_ref, x_ref, g_ref, shift_ref, scale_ref, i // blocks_per_batch, rows)

    z = _dot(h_ref[...], w_ref[...])
    z_ref[...] = 0.5 * z * (1.0 + lax.erf(z * (2.0 ** -0.5)))


def _gmlp_proj(x2d, mods, norm_g3, w_in, layer, jm, seq, tm=1024, tn=512, rows=128):
    m, d = x2d.shape
    n = w_in.shape[2]
    assert seq % tm == 0 and n % tn == 0
    kern = functools.partial(_gmlp_proj_kernel, blocks_per_batch=seq // tm, rows=rows)
    vmem = _vmem_limit(
        pipelined=_nbytes((tm, d), F32) + _nbytes((d, tn), BF16) + _nbytes((tm, tn), F32)
        + 3 * _nbytes((MOD_ROWS, d), F32),
        single=0, scratch=_nbytes((tm, d), BF16), temps=5 * _nbytes((tm, tn), F32))
    return pl.pallas_call(
        kern,
        grid=(m // tm, n // tn),
        in_specs=[
            pl.BlockSpec((tm, d), lambda i, j: (i, 0)),
            pl.BlockSpec((None, 1, d), lambda i, j: (layer * 3 + 1, 0, 0)),
            _mod_spec(layer, N_MOD + 0, d),
            _mod_spec(layer, N_MOD + 1, d),
            pl.BlockSpec((None, d, tn), lambda i, j: (jm, 0, j)),
        ],
        out_specs=pl.BlockSpec((tm, tn), lambda i, j: (i, j)),
        out_shape=jax.ShapeDtypeStruct((m, n), F32),
        scratch_shapes=[pltpu.VMEM((tm, d), BF16)],
        compiler_params=pltpu.CompilerParams(
            dimension_semantics=("parallel", "arbitrary"), vmem_limit_bytes=vmem),
        name="gmlp_proj",
    )(x2d, norm_g3, mods, mods, w_in)


def _layer_norm_rows(v, g, b):
    mu = jnp.mean(v, axis=-1, keepdims=True)
    dv = v - mu
    var = jnp.mean(dv * dv, axis=-1, keepdims=True)
    return dv * lax.rsqrt(var + EPS) * g + b


def _gmlp_mix_kernel(x_ref, u_ref, v_ref, lng_ref, lnb_ref, ws_ref, bs_ref, gate_ref, wo_ref, o_ref, s_ref,
                     *, blocks_per_batch, n_chunk):
    b = pl.program_id(0) // blocks_per_batch
    tm, e = s_ref.shape
    n_heads, length, _ = ws_ref.shape
    dh = e // n_heads
    lng = lng_ref[...]
    lnb = lnb_ref[...]
    causal = (lax.broadcasted_iota(jnp.int32, (length, length), 0)
              >= lax.broadcasted_iota(jnp.int32, (length, length), 1))

    def chunk(c, carry):
        rs = pl.ds(pl.multiple_of(c * length, length), length)
        vn = _layer_norm_rows(v_ref[rs, :], lng, lnb).astype(BF16)
        for hd in range(n_heads):
            cs = slice(hd * dh, (hd + 1) * dh)
            w_h = jnp.where(causal, ws_ref[hd], 0.0).astype(BF16)
            mixed = _dot(w_h, vn[:, cs]) + bs_ref[hd]
            s_ref[rs, cs] = (u_ref[rs, cs] * mixed).astype(BF16)
        return carry

    lax.fori_loop(0, tm // length, chunk, 0)
    gate = gate_ref[pl.ds(b, 1), :]
    s = s_ref[...]
    for n0 in range(0, o_ref.shape[1], n_chunk):
        ns = slice(n0, n0 + n_chunk)
        o_ref[:, ns] = x_ref[:, ns] + gate[:, ns] * _dot(s, wo_ref[:, ns])


def _gmlp_mix(x2d, z, mods, ln_g, ln_b, ws, bs, w_out, layer, jm, seq, tm=512, n_chunk=512):
    m, d = x2d.shape
    e = w_out.shape[1]
    n_heads, length = ws.shape[1], ws.shape[2]
    assert length == CHUNK and seq % tm == 0 and tm % length == 0 and (e // n_heads) % V7X_LANES == 0
    assert d % n_chunk == 0
    kern = functools.partial(_gmlp_mix_kernel, blocks_per_batch=seq // tm, n_chunk=n_chunk)
    vmem = _vmem_limit(
        pipelined=_nbytes((tm, d), F32) * 2 + 2 * _nbytes((tm, e), F32) + 2 * _nbytes((n_heads, length, length), F32)
        + 3 * _nbytes((MOD_ROWS, d), F32),
        single=_nbytes((e, d), BF16), scratch=_nbytes((tm, e), BF16),
        temps=3 * _nbytes((tm, n_chunk), F32) + 6 * _nbytes((length, e), F32))
    return pl.pallas_call(
        kern,
        grid=(m // tm,),
        in_specs=[
            pl.BlockSpec((tm, d), lambda i: (i, 0)),
            pl.BlockSpec((tm, e), lambda i: (i, 0)),
            pl.BlockSpec((tm, e), lambda i: (i, 1)),
            pl.BlockSpec((None, 1, e), lambda i: (jm, 0, 0)),
            pl.BlockSpec((None, 1, e), lambda i: (jm, 0, 0)),
            pl.BlockSpec((None, n_heads, length, length), lambda i: (jm, 0, 0, 0)),
            pl.BlockSpec((None, n_heads, length, 1), lambda i: (jm, 0, 0, 0)),
            _mod_spec(layer, N_MOD + 2, d),
            pl.BlockSpec((None, e, d), lambda i: (jm, 0, 0), pipeline_mode=pl.Buffered(1)),
        ],
        out_specs=pl.BlockSpec((tm, d), lambda i: (i, 0)),
        out_shape=jax.ShapeDtypeStruct((m, d), F32),
        scratch_shapes=[pltpu.VMEM((tm, e), BF16)],
        compiler_params=pltpu.CompilerParams(dimension_semantics=("parallel",), vmem_limit_bytes=vmem),
        name="gmlp_mix",
    )(x2d, z, z, ln_g[:, None, :], ln_b[:, None, :], ws, bs[..., None], mods, w_out)


def _conv_proj_kernel(x_ref, g_ref, shift_ref, scale_ref, wa_ref, wg_ref, ba_ref, bg_ref, y_ref, h_ref,
                      *, blocks_per_batch, rows):
    i = pl.program_id(0)

    @pl.when(pl.program_id(1) == 0)
    def _():
        _fill_h(h_ref, x_ref, g_ref, shift_ref, scale_ref, i // blocks_per_batch, rows)

    h = h_ref[...]
    a = _dot(h, wa_ref[...]) + ba_ref[...]
    gl = _dot(h, wg_ref[...]) + bg_ref[...]
    y_ref[...] = a * jax.nn.sigmoid(gl)


def _conv_proj(x2d, mods, norm_g3, w_in, b_in, layer, jm, seq, tm=1024, tn=512, rows=128):
    m, d = x2d.shape
    c = w_in.shape[2] // 2
    assert seq % tm == 0 and c % tn == 0
    n_j = c // tn
    kern = functools.partial(_conv_proj_kernel, blocks_per_batch=seq // tm, rows=rows)
    vmem = _vmem_limit(
        pipelined=_nbytes((tm, d), F32) + 2 * _nbytes((d, tn), BF16) + _nbytes((tm, tn), F32)
        + 3 * _nbytes((MOD_ROWS, d), F32) + 2 * _nbytes((MOD_ROWS, tn), F32),
        single=0, scratch=_nbytes((tm, d), BF16), temps=6 * _nbytes((tm, tn), F32))
    return pl.pallas_call(
        kern,
        grid=(m // tm, n_j),
        in_specs=[
            pl.BlockSpec((tm, d), lambda i, j: (i, 0)),
            pl.BlockSpec((None, 1, d), lambda i, j: (layer * 3 + 1, 0, 0)),
            _mod_spec(layer, N_MOD + 0, d),
            _mod_spec(layer, N_MOD + 1, d),
            pl.BlockSpec((None, d, tn), lambda i, j: (jm, 0, j)),
            pl.BlockSpec((None, d, tn), lambda i, j: (jm, 0, j + n_j)),
            pl.BlockSpec((None, 1, tn), lambda i, j: (jm, 0, j)),
            pl.BlockSpec((None, 1, tn), lambda i, j: (jm, 0, j + n_j)),
        ],
        out_specs=pl.BlockSpec((tm, tn), lambda i, j: (i, j)),
        out_shape=jax.ShapeDtypeStruct((m, c), F32),
        scratch_shapes=[pltpu.VMEM((tm, d), BF16)],
        compiler_params=pltpu.CompilerParams(
            dimension_semantics=("parallel", "arbitrary"), vmem_limit_bytes=vmem),
        name="conv_proj",
    )(x2d, norm_g3, mods, mods, w_in, w_in, b_in[:, None, :], b_in[:, None, :])


def _conv_mix_kernel(x_ref, y_ref, yprev_ref, dww_ref, dwb_ref, lng_ref, lnb_ref, gate_ref, wo_ref, bo_ref,
                     o_ref, win_ref, yc_ref, s_ref, *, blocks_per_batch, rows, n_chunk):
    i = pl.program_id(0)
    b = i // blocks_per_batch
    tm, c = y_ref.shape
    k_taps = dww_ref.shape[0]
    first = (i % blocks_per_batch) == 0

    win_ref[0:CONV_HALO, :] = jnp.where(first, 0.0, yprev_ref[...])
    win_ref[CONV_HALO:, :] = y_ref[...]

    lng = lng_ref[...]
    lnb = lnb_ref[...]
    base = CONV_HALO - (k_taps - 1)

    def chunk(r, carry):
        r0 = pl.multiple_of(r * rows, rows)
        for s in range(c // V7X_LANES):
            cs = slice(s * V7X_LANES, (s + 1) * V7X_LANES)
            acc = jnp.broadcast_to(dwb_ref[:, cs], (rows, V7X_LANES))
            for p in range(SUBLANES):
                part = None
                for a in range(CONV_HALO // SUBLANES + 1):
                    k = a * SUBLANES + p - base
                    if 0 <= k < k_taps:
                        n_load = rows + (SUBLANES if p else 0)
                        term = dww_ref[k:k + 1, cs] * win_ref[pl.ds(r0 + a * SUBLANES, n_load), cs]
                        part = term if part is None else part + term
                acc = acc + part[p:p + rows]
            yc_ref[pl.ds(r0, rows), cs] = acc
        yl = _layer_norm_rows(yc_ref[pl.ds(r0, rows), :], lng, lnb)
        s_ref[pl.ds(r0, rows), :] = (yl * jax.nn.sigmoid(yl)).astype(BF16)
        return carry

    lax.fori_loop(0, tm // rows, chunk, 0)
    gate = gate_ref[pl.ds(b, 1), :]
    s = s_ref[...]
    for n0 in range(0, o_ref.shape[1], n_chunk):
        ns = slice(n0, n0 + n_chunk)
        o_ref[:, ns] = x_ref[:, ns] + gate[:, ns] * (_dot(s, wo_ref[:, ns]) + bo_ref[:, ns])


def _conv_mix(x2d, y, mods, dw_w, dw_b, ln_g, ln_b, w_out, b_out, layer, jm, seq, tm=512, rows=64,
              n_chunk=512):
    m, d = x2d.shape
    c = y.shape[1]
    k_taps = dw_w.shape[1]
    assert k_taps - 1 <= CONV_HALO and seq % tm == 0 and tm % CONV_HALO == 0 and tm % rows == 0
    assert d % n_chunk == 0
    per = tm // CONV_HALO
    kern = functools.partial(_conv_mix_kernel, blocks_per_batch=seq // tm, rows=rows, n_chunk=n_chunk)
    vmem = _vmem_limit(
        pipelined=2 * _nbytes((tm, d), F32) + _nbytes((tm, c), F32) + _nbytes((CONV_HALO, c), F32)
        + _nbytes((CONV_HALO, c), F32) + 6 * _nbytes((MOD_ROWS, d), F32),
        single=_nbytes((c, d), BF16),
        scratch=_nbytes((tm + CONV_HALO, c), F32) + _nbytes((tm, c), F32) + _nbytes((tm, c), BF16),
        temps=4 * _nbytes((tm, n_chunk), F32) + 8 * _nbytes((rows, c), F32))
    return pl.pallas_call(
        kern,
        grid=(m // tm,),
        in_specs=[
            pl.BlockSpec((tm, d), lambda i: (i, 0)),
            pl.BlockSpec((tm, c), lambda i: (i, 0)),
            pl.BlockSpec((CONV_HALO, c), lambda i: (jnp.maximum(i * per - 1, 0), 0)),
            pl.BlockSpec((None, k_taps, c), lambda i: (jm, 0, 0)),
            pl.BlockSpec((None, 1, c), lambda i: (jm, 0, 0)),
            pl.BlockSpec((None, 1, c), lambda i: (jm, 0, 0)),
            pl.BlockSpec((None, 1, c), lambda i: (jm, 0, 0)),
            _mod_spec(layer, N_MOD + 2, d),
            pl.BlockSpec((None, c, d), lambda i: (jm, 0, 0), pipeline_mode=pl.Buffered(1)),
            pl.BlockSpec((None, 1, d), lambda i: (jm, 0, 0)),
        ],
        out_specs=pl.BlockSpec((tm, d), lambda i: (i, 0)),
        out_shape=jax.ShapeDtypeStruct((m, d), F32),
        scratch_shapes=[pltpu.VMEM((tm + CONV_HALO, c), F32), pltpu.VMEM((tm, c), F32), pltpu.VMEM((tm, c), BF16)],
        compiler_params=pltpu.CompilerParams(dimension_semantics=("parallel",), vmem_limit_bytes=vmem),
        name="conv_mix",
    )(x2d, y, y, dw_w, dw_b[:, None, :], ln_g[:, None, :], ln_b[:, None, :], mods, w_out, b_out[:, None, :])


def kernel(x, c, ada_w, ada_b, norm_g, ffn_w_in, ffn_w_out, gm_w_in, gm_ln_g, gm_ln_b, gm_ws, gm_bs, gm_w_out,
           cv_w_in, cv_b_in, cv_dw_w, cv_dw_b, cv_ln_g, cv_ln_b, cv_w_out, cv_b_out, final_g):
    bsz, seq, d = x.shape
    depth, n_sub = norm_g.shape[0], norm_g.shape[1]
    n_mixers = 2
    assert n_sub == 3 and ada_w.shape[2] == n_sub * N_MOD * d

    mods = _ada_table(c, ada_w, ada_b)
    norm_g3 = norm_g.reshape(depth * n_sub, 1, d)
    gm_w_in, gm_w_out = gm_w_in.astype(BF16), gm_w_out.astype(BF16)
    cv_w_in, cv_w_out = cv_w_in.astype(BF16), cv_w_out.astype(BF16)

    h = x.reshape(bsz * seq, d)
    for i in range(depth):
        h = _ffn(h, mods, norm_g3, ffn_w_in, ffn_w_out, i, 0, 0, seq)
        jm = i // n_mixers
        if i % n_mixers == 0:
            z = _gmlp_proj(h, mods, norm_g3, gm_w_in, i, jm, seq)
            h = _gmlp_mix(h, z, mods, gm_ln_g, gm_ln_b, gm_ws, gm_bs, gm_w_out, i, jm, seq)
        else:
            y = _conv_proj(h, mods, norm_g3, cv_w_in, cv_b_in, i, jm, seq)
            h = _conv_mix(h, y, mods, cv_dw_w, cv_dw_b, cv_ln_g, cv_ln_b, cv_w_out, cv_b_out, i, jm, seq)
        h = _ffn(h, mods, norm_g3, ffn_w_in, ffn_w_out, i, 2, 1, seq,
                 final_g=final_g if i == depth - 1 else None)
    return h.reshape(bsz, seq, d)
```

```python
import functools

import jax
import jax.numpy as jnp
from jax import lax
from jax.experimental import pallas as pl
from jax.experimental.pallas import tpu as pltpu

EPS = 1e-6
N_MOD = 3
CHUNK = 128
MOD_ROWS = 8
CONV_HALO = 32

V7X_VMEM_BYTES = 64 * 1024 * 1024
V7X_LANES = 128
SUBLANES = 8

F32 = jnp.float32
BF16 = jnp.bfloat16


def _vmem_limit(pipelined, single, scratch, temps):
    need = 2 * pipelined + single + scratch + temps
    cap = V7X_VMEM_BYTES - 1024 * 1024
    assert need <= cap, need
    return int(min(need + need // 10, cap))


def _nbytes(shape, dtype):
    n = 1
    for s in shape:
        n *= s
    return n * jnp.dtype(dtype).itemsize


def _dot(a, b):
    return jnp.dot(a, b, preferred_element_type=F32)


def _norm_mod_rows(x, g, shift, scale):
    ms = jnp.mean(x * x, axis=-1, keepdims=True)
    y = x * lax.rsqrt(ms + EPS) * g
    return y * (1.0 + scale) + shift


def _fill_h(h_ref, x_ref, g_ref, shift_ref, scale_ref, b, rows):
    tm = x_ref.shape[0]
    g = g_ref[...]
    shift = shift_ref[pl.ds(b, 1), :]
    scale = scale_ref[pl.ds(b, 1), :]

    def body(r, carry):
        rs = pl.ds(pl.multiple_of(r * rows, rows), rows)
        h_ref[rs, :] = _norm_mod_rows(x_ref[rs, :], g, shift, scale).astype(BF16)
        return carry

    lax.fori_loop(0, tm // rows, body, 0)


def _ada_kernel(c_ref, w_ref, b_ref, o_ref):
    c = c_ref[...]
    cond = (c * jax.nn.sigmoid(c)).astype(BF16)
    o_ref[...] = _dot(cond, w_ref[...].astype(BF16)) + b_ref[...]


def _ada_table(c, ada_w, ada_b, tn=1024):
    depth, d, n = ada_w.shape
    bsz = c.shape[0]
    assert bsz <= MOD_ROWS and n % d == 0 and d % tn == 0
    per = d // tn
    c8 = jnp.pad(c, ((0, MOD_ROWS - bsz), (0, 0)))
    b4 = ada_b.reshape(depth, n // d, 1, d)
    return pl.pallas_call(
        _ada_kernel,
        grid=(depth, n // tn),
        in_specs=[
            pl.BlockSpec((MOD_ROWS, d), lambda l, j: (0, 0)),
            pl.BlockSpec((None, d, tn), lambda l, j: (l, 0, j)),
            pl.BlockSpec((None, None, 1, tn), lambda l, j: (l, j // per, 0, j % per)),
        ],
        out_specs=pl.BlockSpec((None, None, MOD_ROWS, tn), lambda l, j: (l, j // per, 0, j % per)),
        out_shape=jax.ShapeDtypeStruct((depth, n // d, MOD_ROWS, d), F32),
        compiler_params=pltpu.CompilerParams(
            dimension_semantics=("parallel", "parallel"),
            vmem_limit_bytes=_vmem_limit(
                _nbytes((d, tn), F32) + _nbytes((MOD_ROWS, d + 2 * tn), F32), 0, 0,
                2 * _nbytes((d, tn), BF16) + 4 * _nbytes((MOD_ROWS, d), F32))),
        name="ada",
    )(c8, ada_w, b4)


def _mod_spec(layer, row, d):
    return pl.BlockSpec((None, None, MOD_ROWS, d), lambda *_: (layer, row, 0, 0))


def _ffn_kernel(x_ref, g_ref, shift_ref, scale_ref, gate_ref, wg_ref, wu_ref, wo_ref, *rest,
                blocks_per_batch, n_f, n_chunk, rows, final):
    if final:
        fg_ref, o_ref, h_ref = rest
    else:
        o_ref, h_ref = rest
    i = pl.program_id(0)
    f = pl.program_id(1)
    b = i // blocks_per_batch
    tm, d = o_ref.shape

    @pl.when(f == 0)
    def _():
        _fill_h(h_ref, x_ref, g_ref, shift_ref, scale_ref, b, rows)
        o_ref[...] = x_ref[...]

    half_gate = 0.5 * gate_ref[pl.ds(b, 1), :]
    h = h_ref[...]
    gt = _dot(h, wg_ref[...].astype(BF16))
    up = _dot(h, wu_ref[...].astype(BF16))
    a = (gt * jax.nn.sigmoid(gt) * up).astype(BF16)
    for n0 in range(0, d, n_chunk):
        ns = slice(n0, n0 + n_chunk)
        o_ref[:, ns] += half_gate[:, ns] * _dot(a, wo_ref[:, ns].astype(BF16))

    if final:
        @pl.when(f == n_f - 1)
        def _():
            fg = fg_ref[...]

            def body(r, carry):
                rs = pl.ds(pl.multiple_of(r * rows, rows), rows)
                o = o_ref[rs, :]
                ms = jnp.mean(o * o, axis=-1, keepdims=True)
                o_ref[rs, :] = o * lax.rsqrt(ms + EPS) * fg
                return carry

            lax.fori_loop(0, tm // rows, body, 0)


def _ffn(x2d, mods, norm_g3, w_in, w_out, layer, sub, which, seq, final_g=None, tm=1024, tf=512,
         n_chunk=512, rows=128):
    m, d = x2d.shape
    f_dim = w_out.shape[2]
    assert seq % tm == 0 and f_dim % tf == 0 and d % n_chunk == 0 and tm % rows == 0
    n_f = f_dim // tf
    row0 = sub * N_MOD
    in_specs = [
        pl.BlockSpec((tm, d), lambda i, f: (i, 0), pipeline_mode=pl.Buffered(1)),
        pl.BlockSpec((None, 1, d), lambda i, f: (layer * 3 + sub, 0, 0)),
        _mod_spec(layer, row0 + 0, d),
        _mod_spec(layer, row0 + 1, d),
        _mod_spec(layer, row0 + 2, d),
        pl.BlockSpec((None, None, d, tf), lambda i, f: (layer, which, 0, f)),
        pl.BlockSpec((None, None, d, tf), lambda i, f: (layer, which, 0, f + n_f)),
        pl.BlockSpec((None, None, tf, d), lambda i, f: (layer, which, f, 0)),
    ]
    args = [x2d, norm_g3, mods, mods, mods, w_in, w_in, w_out]
    if final_g is not None:
        in_specs.append(pl.BlockSpec((1, d), lambda i, f: (0, 0)))
        args.append(final_g.reshape(1, d))
    kern = functools.partial(_ffn_kernel, blocks_per_batch=seq // tm, n_f=n_f, n_chunk=n_chunk,
                             rows=rows, final=final_g is not None)
    vmem = _vmem_limit(
        pipelined=_nbytes((tm, d), F32) + 3 * _nbytes((d, tf), w_in.dtype) + 5 * _nbytes((MOD_ROWS, d), F32),
        single=_nbytes((tm, d), F32),
        scratch=_nbytes((tm, d), BF16),
        temps=2 * _nbytes((tm, tf), F32) + _nbytes((tm, tf), BF16))
    return pl.pallas_call(
        kern,
        grid=(m // tm, n_f),
        in_specs=in_specs,
        out_specs=pl.BlockSpec((tm, d), lambda i, f: (i, 0)),
        out_shape=jax.ShapeDtypeStruct((m, d), F32),
        scratch_shapes=[pltpu.VMEM((tm, d), BF16)],
        compiler_params=pltpu.CompilerParams(
            dimension_semantics=("parallel", "arbitrary"), vmem_limit_bytes=vmem),
        name="ffn",
    )(*args)


def _gmlp_proj_kernel(x_ref, g_ref, shift_ref, scale_ref, w_ref, z_ref, h_ref, *, blocks_per_batch, rows):
    i = pl.program_id(0)

    @pl.when(pl.program_id(1) == 0)
    def _():
        _fill_h(h_ref, x_ref, g_ref, shift_ref, scale_ref, i // blocks_per_batch, rows)

    z = _dot(h_ref[...], w_ref[...])
    z_ref[...] = 0.5 * z * (1.0 + lax.erf(z * (2.0 ** -0.5)))


def _gmlp_proj(x2d, mods, norm_g3, w_in, layer, jm, seq, tm=1024, tn=512, rows=128):
    m, d = x2d.shape
    n = w_in.shape[2]
    assert seq % tm == 0 and n % tn == 0
    kern = functools.partial(_gmlp_proj_kernel, blocks_per_batch=seq // tm, rows=rows)
    vmem = _vmem_limit(
        pipelined=_nbytes((tm, d), F32) + _nbytes((d, tn), BF16) + _nbytes((tm, tn), F32)
        + 3 * _nbytes((MOD_ROWS, d), F32),
        single=0, scratch=_nbytes((tm, d), BF16), temps=5 * _nbytes((tm, tn), F32))
    return pl.pallas_call(
        kern,
        grid=(m // tm, n // tn),
        in_specs=[
            pl.BlockSpec((tm, d), lambda i, j: (i, 0)),
            pl.BlockSpec((None, 1, d), lambda i, j: (layer * 3 + 1, 0, 0)),
            _mod_spec(layer, N_MOD + 0, d),
            _mod_spec(layer, N_MOD + 1, d),
            pl.BlockSpec((None, d, tn), lambda i, j: (jm, 0, j)),
        ],
        out_specs=pl.BlockSpec((tm, tn), lambda i, j: (i, j)),
        out_shape=jax.ShapeDtypeStruct((m, n), F32),
        scratch_shapes=[pltpu.VMEM((tm, d), BF16)],
        compiler_params=pltpu.CompilerParams(
            dimension_semantics=("parallel", "arbitrary"), vmem_limit_bytes=vmem),
        name="gmlp_proj",
    )(x2d, norm_g3, mods, mods, w_in)


def _layer_norm_rows(v, g, b):
    mu = jnp.mean(v, axis=-1, keepdims=True)
    dv = v - mu
    var = jnp.mean(dv * dv, axis=-1, keepdims=True)
    return dv * lax.rsqrt(var + EPS) * g + b


def _gmlp_mix_kernel(x_ref, u_ref, v_ref, lng_ref, lnb_ref, ws_ref, bs_ref, gate_ref, wo_ref, o_ref, s_ref,
                     *, blocks_per_batch, n_chunk):
    b = pl.program_id(0) // blocks_per_batch
    tm, e = s_ref.shape
    n_heads, length, _ = ws_ref.shape
    dh = e // n_heads
    lng = lng_ref[...]
    lnb = lnb_ref[...]
    causal = (lax.broadcasted_iota(jnp.int32, (length, length), 0)
              >= lax.broadcasted_iota(jnp.int32, (length, length), 1))

    def chunk(c, carry):
        rs = pl.ds(pl.multiple_of(c * length, length), length)
        vn = _layer_norm_rows(v_ref[rs, :], lng, lnb).astype(BF16)
        for hd in range(n_heads):
            cs = slice(hd * dh, (hd + 1) * dh)
            w_h = jnp.where(causal, ws_ref[hd], 0.0).astype(BF16)
            mixed = _dot(w_h, vn[:, cs]) + bs_ref[hd]
            s_ref[rs, cs] = (u_ref[rs, cs] * mixed).astype(BF16)
        return carry

    lax.fori_loop(0, tm // length, chunk, 0)
    gate = gate_ref[pl.ds(b, 1), :]
    s = s_ref[...]
    for n0 in range(0, o_ref.shape[1], n_chunk):
        ns = slice(n0, n0 + n_chunk)
        o_ref[:, ns] = x_ref[:, ns] + gate[:, ns] * _dot(s, wo_ref[:, ns])


def _gmlp_mix(x2d, z, mods, ln_g, ln_b, ws, bs, w_out, layer, jm, seq, tm=512, n_chunk=512):
    m, d = x2d.shape
    e = w_out.shape[1]
    n_heads, length = ws.shape[1], ws.shape[2]
    assert length == CHUNK and seq % tm == 0 and tm % length == 0 and (e // n_heads) % V7X_LANES == 0
    assert d % n_chunk == 0
    kern = functools.partial(_gmlp_mix_kernel, blocks_per_batch=seq // tm, n_chunk=n_chunk)
    vmem = _vmem_limit(
        pipelined=_nbytes((tm, d), F32) * 2 + 2 * _nbytes((tm, e), F32) + 2 * _nbytes((n_heads, length, length), F32)
        + 3 * _nbytes((MOD_ROWS, d), F32),
        single=_nbytes((e, d), BF16), scratch=_nbytes((tm, e), BF16),
        temps=3 * _nbytes((tm, n_chunk), F32) + 6 * _nbytes((length, e), F32))
    return pl.pallas_call(
        kern,
        grid=(m // tm,),
        in_specs=[
            pl.BlockSpec((tm, d), lambda i: (i, 0)),
            pl.BlockSpec((tm, e), lambda i: (i, 0)),
            pl.BlockSpec((tm, e), lambda i: (i, 1)),
            pl.BlockSpec((None, 1, e), lambda i: (jm, 0, 0)),
            pl.BlockSpec((None, 1, e), lambda i: (jm, 0, 0)),
            pl.BlockSpec((None, n_heads, length, length), lambda i: (jm, 0, 0, 0)),
            pl.BlockSpec((None, n_heads, length, 1), lambda i: (jm, 0, 0, 0)),
            _mod_spec(layer, N_MOD + 2, d),
            pl.BlockSpec((None, e, d), lambda i: (jm, 0, 0), pipeline_mode=pl.Buffered(1)),
        ],
        out_specs=pl.BlockSpec((tm, d), lambda i: (i, 0)),
        out_shape=jax.ShapeDtypeStruct((m, d), F32),
        scratch_shapes=[pltpu.VMEM((tm, e), BF16)],
        compiler_params=pltpu.CompilerParams(dimension_semantics=("parallel",), vmem_limit_bytes=vmem),
        name="gmlp_mix",
    )(x2d, z, z, ln_g[:, None, :], ln_b[:, None, :], ws, bs[..., None], mods, w_out)


def _conv_proj_kernel(x_ref, g_ref, shift_ref, scale_ref, wa_ref, wg_ref, ba_ref, bg_ref, y_ref, h_ref,
                      *, blocks_per_batch, rows):
    i = pl.program_id(0)

    @pl.when(pl.program_id(1) == 0)
    def _():
        _fill_h(h_ref, x_ref, g_ref, shift_ref, scale_ref, i // blocks_per_batch, rows)

    h = h_ref[...]
    a = _dot(h, wa_ref[...]) + ba_ref[...]
    gl = _dot(h, wg_ref[...]) + bg_ref[...]
    y_ref[...] = a * jax.nn.sigmoid(gl)


def _conv_proj(x2d, mods, norm_g3, w_in, b_in, layer, jm, seq, tm=1024, tn=512, rows=128):
    m, d = x2d.shape
    c = w_in.shape[2] // 2
    assert seq % tm == 0 and c % tn == 0
    n_j = c // tn
    kern = functools.partial(_conv_proj_kernel, blocks_per_batch=seq // tm, rows=rows)
    vmem = _vmem_limit(
        pipelined=_nbytes((tm, d), F32) + 2 * _nbytes((d, tn), BF16) + _nbytes((tm, tn), F32)
        + 3 * _nbytes((MOD_ROWS, d), F32) + 2 * _nbytes((MOD_ROWS, tn), F32),
        single=0, scratch=_nbytes((tm, d), BF16), temps=6 * _nbytes((tm, tn), F32))
    return pl.pallas_call(
        kern,
        grid=(m // tm, n_j),
        in_specs=[
            pl.BlockSpec((tm, d), lambda i, j: (i, 0)),
            pl.BlockSpec((None, 1, d), lambda i, j: (layer * 3 + 1, 0, 0)),
            _mod_spec(layer, N_MOD + 0, d),
            _mod_spec(layer, N_MOD + 1, d),
            pl.BlockSpec((None, d, tn), lambda i, j: (jm, 0, j)),
            pl.BlockSpec((None, d, tn), lambda i, j: (jm, 0, j + n_j)),
            pl.BlockSpec((None, 1, tn), lambda i, j: (jm, 0, j)),
            pl.BlockSpec((None, 1, tn), lambda i, j: (jm, 0, j + n_j)),
        ],
        out_specs=pl.BlockSpec((tm, tn), lambda i, j: (i, j)),
        out_shape=jax.ShapeDtypeStruct((m, c), F32),
        scratch_shapes=[pltpu.VMEM((tm, d), BF16)],
        compiler_params=pltpu.CompilerParams(
            dimension_semantics=("parallel", "arbitrary"), vmem_limit_bytes=vmem),
        name="conv_proj",
    )(x2d, norm_g3, mods, mods, w_in, w_in, b_in[:, None, :], b_in[:, None, :])


def _conv_mix_kernel(x_ref, y_ref, yprev_ref, dww_ref, dwb_ref, lng_ref, lnb_ref, gate_ref, wo_ref, bo_ref,
                     o_ref, win_ref, yc_ref, s_ref, *, blocks_per_batch, rows, n_chunk):
    i = pl.program_id(0)
    b = i // blocks_per_batch
    tm, c = y_ref.shape
    k_taps = dww_ref.shape[0]
    first = (i % blocks_per_batch) == 0

    win_ref[0:CONV_HALO, :] = jnp.where(first, 0.0, yprev_ref[...])
    win_ref[CONV_HALO:, :] = y_ref[...]

    lng = lng_ref[...]
    lnb = lnb_ref[...]
    base = CONV_HALO - (k_taps - 1)

    def chunk(r, carry):
        r0 = pl.multiple_of(r * rows, rows)
        for s in range(c // V7X_LANES):
            cs = slice(s * V7X_LANES, (s + 1) * V7X_LANES)
            acc = jnp.broadcast_to(dwb_ref[:, cs], (rows, V7X_LANES))
            for p in range(SUBLANES):
                part = None
                for a in range(CONV_HALO // SUBLANES + 1):
                    k = a * SUBLANES + p - base
                    if 0 <= k < k_taps:
                        n_load = rows + (SUBLANES if p else 0)
                        term = dww_ref[k:k + 1, cs] * win_ref[pl.ds(r0 + a * SUBLANES, n_load), cs]
                        part = term if part is None else part + term
                acc = acc + part[p:p + rows]
            yc_ref[pl.ds(r0, rows), cs] = acc
        yl = _layer_norm_rows(yc_ref[pl.ds(r0, rows), :], lng, lnb)
        s_ref[pl.ds(r0, rows), :] = (yl * jax.nn.sigmoid(yl)).astype(BF16)
        return carry

    lax.fori_loop(0, tm // rows, chunk, 0)
    gate = gate_ref[pl.ds(b, 1), :]
    s = s_ref[...]
    for n0 in range(0, o_ref.shape[1], n_chunk):
        ns = slice(n0, n0 + n_chunk)
        o_ref[:, ns] = x_ref[:, ns] + gate[:, ns] * (_dot(s, wo_ref[:, ns]) + bo_ref[:, ns])


def _conv_mix(x2d, y, mods, dw_w, dw_b, ln_g, ln_b, w_out, b_out, layer, jm, seq, tm=512, rows=64,
              n_chunk=512):
    m, d = x2d.shape
    c = y.shape[1]
    k_taps = dw_w.shape[1]
    assert k_taps - 1 <= CONV_HALO and seq % tm == 0 and tm % CONV_HALO == 0 and tm % rows == 0
    assert d % n_chunk == 0
    per = tm // CONV_HALO
    kern = functools.partial(_conv_mix_kernel, blocks_per_batch=seq // tm, rows=rows, n_chunk=n_chunk)
    vmem = _vmem_limit(
        pipelined=2 * _nbytes((tm, d), F32) + _nbytes((tm, c), F32) + _nbytes((CONV_HALO, c), F32)
        + _nbytes((CONV_HALO, c), F32) + 6 * _nbytes((MOD_ROWS, d), F32),
        single=_nbytes((c, d), BF16),
        scratch=_nbytes((tm + CONV_HALO, c), F32) + _nbytes((tm, c), F32) + _nbytes((tm, c), BF16),
        temps=4 * _nbytes((tm, n_chunk), F32) + 8 * _nbytes((rows, c), F32))
    return pl.pallas_call(
        kern,
        grid=(m // tm,),
        in_specs=[
            pl.BlockSpec((tm, d), lambda i: (i, 0)),
            pl.BlockSpec((tm, c), lambda i: (i, 0)),
            pl.BlockSpec((CONV_HALO, c), lambda i: (jnp.maximum(i * per - 1, 0), 0)),
            pl.BlockSpec((None, k_taps, c), lambda i: (jm, 0, 0)),
            pl.BlockSpec((None, 1, c), lambda i: (jm, 0, 0)),
            pl.BlockSpec((None, 1, c), lambda i: (jm, 0, 0)),
            pl.BlockSpec((None, 1, c), lambda i: (jm, 0, 0)),
            _mod_spec(layer, N_MOD + 2, d),
            pl.BlockSpec((None, c, d), lambda i: (jm, 0, 0), pipeline_mode=pl.Buffered(1)),
            pl.BlockSpec((None, 1, d), lambda i: (jm, 0, 0)),
        ],
        out_specs=pl.BlockSpec((tm, d), lambda i: (i, 0)),
        out_shape=jax.ShapeDtypeStruct((m, d), F32),
        scratch_shapes=[pltpu.VMEM((tm + CONV_HALO, c), F32), pltpu.VMEM((tm, c), F32), pltpu.VMEM((tm, c), BF16)],
        compiler_params=pltpu.CompilerParams(dimension_semantics=("parallel",), vmem_limit_bytes=vmem),
        name="conv_mix",
    )(x2d, y, y, dw_w, dw_b[:, None, :], ln_g[:, None, :], ln_b[:, None, :], mods, w_out, b_out[:, None, :])


def _conv_mixer_kernel(xc_ref, xr_ref, g_ref, shift_ref, scale_ref, gate_ref, wi_ref, bi_ref, dww_ref, dwb_ref,
                       lng_ref, lnb_ref, wo_ref, bo_ref, o_ref, h_ref, win_ref, s_ref, op_ref,
                       *, n_blocks, blocks_per_batch, rows):
    step = pl.program_id(0)
    _, n_slabs, _, lanes = win_ref.shape
    tm = s_ref.shape[0]
    n_tiles, _, tn = op_ref.shape
    yc_ref = win_ref.at[2]
    k_taps = dww_ref.shape[1]
    base = CONV_HALO - (k_taps - 1)
    slabs_per_tile = tn // lanes
    rd = step % 2

    @pl.when(step == 0)
    def _():
        win_ref[...] = jnp.zeros_like(win_ref)
        s_ref[...] = jnp.zeros_like(s_ref)

    b_in = jnp.minimum(step, n_blocks - 1) // blocks_per_batch
    h_ref[...] = _norm_mod_rows(xc_ref[...], g_ref[...], shift_ref[pl.ds(b_in, 1), :],
                                scale_ref[pl.ds(b_in, 1), :]).astype(BF16)

    def conv_slab(sl):
        for r0 in range(0, tm, rows):
            acc = jnp.broadcast_to(dwb_ref[sl], (rows, lanes))
            for p in range(SUBLANES):
                part = None
                for a8 in range(0, CONV_HALO + SUBLANES, SUBLANES):
                    k = a8 + p - base
                    if 0 <= k < k_taps:
                        n_load = rows + (SUBLANES if p else 0)
                        term = dww_ref[sl, k:k + 1, :] * win_ref[rd, sl, r0 + a8:r0 + a8 + n_load, :]
                        part = term if part is None else part + term
                acc = acc + part[p:p + rows]
            yc_ref[sl, r0:r0 + rows, :] = acc

    def tile(j, carry):
        for t in range(slabs_per_tile):
            conv_slab(j * slabs_per_tile + t)
        h = h_ref[...]
        op_ref[j] = _dot(s_ref[...], wo_ref[j])
        a = _dot(h, wi_ref[j]) + bi_ref[j]
        gl = _dot(h, wi_ref[n_tiles + j]) + bi_ref[n_tiles + j]
        y = a * jax.nn.sigmoid(gl)
        for t in range(slabs_per_tile):
            win_ref[1 - rd, j * slabs_per_tile + t, CONV_HALO:, :] = y[:, t * lanes:(t + 1) * lanes]
        return carry

    lax.fori_loop(0, n_tiles, tile, 0)

    blk_a = jnp.maximum(step - 2, 0)
    gate = gate_ref[pl.ds(blk_a // blocks_per_batch, 1), :]
    for j in range(n_tiles):
        ns = slice(j * tn, (j + 1) * tn)
        o_ref[:, ns] = xr_ref[:, ns] + gate[:, ns] * (op_ref[j] + bo_ref[:, ns])

    for r0 in range(0, tm, rows):
        tot = yc_ref[0, r0:r0 + rows, :]
        for sl in range(1, n_slabs):
            tot = tot + yc_ref[sl, r0:r0 + rows, :]
        mu = jnp.sum(tot, axis=-1, keepdims=True) * (1.0 / (n_slabs * lanes))
        sq = None
        for sl in range(n_slabs):
            dv = yc_ref[sl, r0:r0 + rows, :] - mu
            sq = dv * dv if sq is None else sq + dv * dv
        var = jnp.sum(sq, axis=-1, keepdims=True) * (1.0 / (n_slabs * lanes))
        rstd = lax.rsqrt(var + EPS)
        for sl in range(n_slabs):
            cs = slice(sl * lanes, (sl + 1) * lanes)
            yl = (yc_ref[sl, r0:r0 + rows, :] - mu) * rstd * lng_ref[:, cs] + lnb_ref[:, cs]
            s_ref[r0:r0 + rows, cs] = (yl * jax.nn.sigmoid(yl)).astype(BF16)

    for sl in range(n_slabs):
        tail = win_ref[rd, sl, tm:tm + CONV_HALO, :]
        win_ref[1 - rd, sl, 0:CONV_HALO, :] = jnp.where(step % blocks_per_batch == 0, 0.0, tail)


def _conv_mixer(x2d, mods, norm_g3, w_in, b_in, dw_w, dw_b, ln_g, ln_b, w_out, b_out, layer, jm, seq,
                tm=256, rows=64, tn=256):
    m, d = x2d.shape
    c = w_out.shape[0]
    k_taps = dw_w.shape[1]
    assert k_taps - 1 <= CONV_HALO and seq % tm == 0 and tm % rows == 0 and c % tn == 0 and d == c
    assert tn % V7X_LANES == 0
    n_blocks, n_tiles, n_slabs = m // tm, c // tn, c // V7X_LANES
    wi3 = w_in.reshape(d, 2 * n_tiles, tn).transpose(1, 0, 2)
    wo3 = w_out.reshape(c, n_tiles, tn).transpose(1, 0, 2)
    bi3 = b_in[jm].reshape(2 * n_tiles, 1, tn)
    dww3 = dw_w[jm].reshape(k_taps, n_slabs, V7X_LANES).transpose(1, 0, 2)
    dwb3 = dw_b[jm].reshape(n_slabs, 1, V7X_LANES)
    kern = functools.partial(_conv_mixer_kernel, n_blocks=n_blocks, blocks_per_batch=seq // tm, rows=rows)
    vmem = _vmem_limit(
        pipelined=3 * _nbytes((tm, d), F32) + 6 * _nbytes((MOD_ROWS, d), F32) + _nbytes((n_slabs, 32, V7X_LANES), F32)
        + _nbytes((2 * n_tiles, MOD_ROWS, tn), F32) + _nbytes((n_slabs, MOD_ROWS, V7X_LANES), F32),
        single=_nbytes((d, 2 * c), BF16) + _nbytes((c, d), BF16),
        scratch=_nbytes((tm, d), BF16) + _nbytes((2 * (tm + CONV_HALO), c), F32) + 2 * _nbytes((tm, c), F32)
        + _nbytes((tm, c), BF16),
        temps=4 * _nbytes((tm, d), F32))
    cur = lambda i: (jnp.minimum(i, n_blocks - 1), 0)
    lag2 = lambda i: (jnp.maximum(i - 2, 0), 0)
    whole = lambda i: (0, 0, 0)
    return pl.pallas_call(
        kern,
        grid=(n_blocks + 2,),
        in_specs=[
            pl.BlockSpec((tm, d), cur),
            pl.BlockSpec((tm, d), lag2),
            pl.BlockSpec((None, 1, d), lambda i: (layer * 3 + 1, 0, 0)),
            _mod_spec(layer, N_MOD + 0, d),
            _mod_spec(layer, N_MOD + 1, d),
            _mod_spec(layer, N_MOD + 2, d),
            pl.BlockSpec((2 * n_tiles, d, tn), whole, pipeline_mode=pl.Buffered(1)),
            pl.BlockSpec((2 * n_tiles, 1, tn), whole),
            pl.BlockSpec((n_slabs, k_taps, V7X_LANES), whole),
            pl.BlockSpec((n_slabs, 1, V7X_LANES), whole),
            pl.BlockSpec((None, 1, c), lambda i: (jm, 0, 0)),
            pl.BlockSpec((None, 1, c), lambda i: (jm, 0, 0)),
            pl.BlockSpec((n_tiles, c, tn), whole, pipeline_mode=pl.Buffered(1)),
            pl.BlockSpec((None, 1, d), lambda i: (jm, 0, 0)),
        ],
        out_specs=pl.BlockSpec((tm, d), lag2),
        out_shape=jax.ShapeDtypeStruct((m, d), F32),
        scratch_shapes=[pltpu.VMEM((tm, d), BF16),
                        pltpu.VMEM((3, n_slabs, tm + CONV_HALO, V7X_LANES), F32),
                        pltpu.VMEM((tm, c), BF16),
                        pltpu.VMEM((n_tiles, tm, tn), F32)],
        compiler_params=pltpu.CompilerParams(dimension_semantics=("arbitrary",), vmem_limit_bytes=vmem),
        name="conv_mixer",
    )(x2d, x2d, norm_g3, mods, mods, mods, wi3, bi3, dww3, dwb3, ln_g[:, None, :], ln_b[:, None, :], wo3,
      b_out[:, None, :])


def _conv_block_kernel(xc_ref, xr_ref, g_ref, shift_ref, scale_ref, gate_ref, wi_ref, bi_ref, dww_ref, dwb_ref,
                       lng_ref, lnb_ref, wo_ref, bo_ref, o_ref, win_ref, yc_ref, s_ref,
                       *, n_blocks, blocks_per_batch, rows, tn):
    step = pl.program_id(0)
    tm, c = yc_ref.shape
    d = o_ref.shape[1]
    k_taps = dww_ref.shape[0]
    base = CONV_HALO - (k_taps - 1)

    @pl.when(step == 0)
    def _():
        win_ref[0:CONV_HALO, :] = jnp.zeros((CONV_HALO, c), F32)
        s_ref[...] = jnp.zeros_like(s_ref)

    b_in = jnp.minimum(step, n_blocks - 1) // blocks_per_batch
    h = _norm_mod_rows(xc_ref[...], g_ref[...], shift_ref[pl.ds(b_in, 1), :],
                       scale_ref[pl.ds(b_in, 1), :]).astype(BF16)
    gate = gate_ref[pl.ds(jnp.maximum(step - 1, 0) // blocks_per_batch, 1), :]
    s_prev = s_ref[...]

    for j0 in range(0, c, tn):
        js = slice(j0, j0 + tn)
        a = _dot(h, wi_ref[:, js]) + bi_ref[:, js]
        gl = _dot(h, wi_ref[:, c + j0:c + j0 + tn]) + bi_ref[:, c + j0:c + j0 + tn]
        win_ref[CONV_HALO:, js] = a * jax.nn.sigmoid(gl)
        o_ref[:, js] = xr_ref[:, js] + gate[:, js] * (_dot(s_prev, wo_ref[:, js]) + bo_ref[:, js])
        for l0 in range(j0, j0 + tn, V7X_LANES):
            cs = slice(l0, l0 + V7X_LANES)
            for r0 in range(0, tm, rows):
                acc = jnp.broadcast_to(dwb_ref[:, cs], (rows, V7X_LANES))
                for p in range(SUBLANES):
                    part = None
                    for a8 in range(0, CONV_HALO + SUBLANES, SUBLANES):
                        k = a8 + p - base
                        if 0 <= k < k_taps:
                            n_load = rows + (SUBLANES if p else 0)
                            term = dww_ref[k:k + 1, cs] * win_ref[r0 + a8:r0 + a8 + n_load, cs]
                            part = term if part is None else part + term
                    acc = acc + part[p:p + rows]
                yc_ref[r0:r0 + rows, cs] = acc

    lng = lng_ref[...]
    lnb = lnb_ref[...]
    for r0 in range(0, tm, rows):
        yl = _layer_norm_rows(yc_ref[r0:r0 + rows, :], lng, lnb)
        s_ref[r0:r0 + rows, :] = (yl * jax.nn.sigmoid(yl)).astype(BF16)

    tail = win_ref[tm:tm + CONV_HALO, :]
    win_ref[0:CONV_HALO, :] = jnp.where((step + 1) % blocks_per_batch == 0, 0.0, tail)


def _conv_block(x2d, mods, norm_g3, w_in, b_in, dw_w, dw_b, ln_g, ln_b, w_out, b_out, layer, jm, seq,
                tm=256, rows=64, tn=256):
    m, d = x2d.shape
    c = w_out.shape[1]
    k_taps = dw_w.shape[1]
    assert k_taps - 1 <= CONV_HALO and seq % tm == 0 and tm % rows == 0 and c % tn == 0 and d == c
    n_blocks = m // tm
    kern = functools.partial(_conv_block_kernel, n_blocks=n_blocks, blocks_per_batch=seq // tm, rows=rows, tn=tn)
    vmem = _vmem_limit(
        pipelined=3 * _nbytes((tm, d), F32) + 8 * _nbytes((MOD_ROWS, d), F32) + _nbytes((CONV_HALO, c), F32)
        + _nbytes((MOD_ROWS, 2 * c), F32),
        single=_nbytes((d, 2 * c), BF16) + _nbytes((c, d), BF16),
        scratch=_nbytes((2 * tm + CONV_HALO, c), F32) + _nbytes((tm, c), BF16),
        temps=_nbytes((tm, d), BF16) + 3 * _nbytes((tm, d), F32))
    cur = lambda i: (jnp.minimum(i, n_blocks - 1), 0)
    lag = lambda i: (jnp.maximum(i - 1, 0), 0)
    return pl.pallas_call(
        kern,
        grid=(n_blocks + 1,),
        in_specs=[
            pl.BlockSpec((tm, d), cur),
            pl.BlockSpec((tm, d), lag),
            pl.BlockSpec((None, 1, d), lambda i: (layer * 3 + 1, 0, 0)),
            _mod_spec(layer, N_MOD + 0, d),
            _mod_spec(layer, N_MOD + 1, d),
            _mod_spec(layer, N_MOD + 2, d),
            pl.BlockSpec((None, d, 2 * c), lambda i: (jm, 0, 0), pipeline_mode=pl.Buffered(1)),
            pl.BlockSpec((None, 1, 2 * c), lambda i: (jm, 0, 0)),
            pl.BlockSpec((None, k_taps, c), lambda i: (jm, 0, 0)),
            pl.BlockSpec((None, 1, c), lambda i: (jm, 0, 0)),
            pl.BlockSpec((None, 1, c), lambda i: (jm, 0, 0)),
            pl.BlockSpec((None, 1, c), lambda i: (jm, 0, 0)),
            pl.BlockSpec((None, c, d), lambda i: (jm, 0, 0), pipeline_mode=pl.Buffered(1)),
            pl.BlockSpec((None, 1, d), lambda i: (jm, 0, 0)),
        ],
        out_specs=pl.BlockSpec((tm, d), lag),
        out_shape=jax.ShapeDtypeStruct((m, d), F32),
        scratch_shapes=[pltpu.VMEM((tm + CONV_HALO, c), F32), pltpu.VMEM((tm, c), F32), pltpu.VMEM((tm, c), BF16)],
        compiler_params=pltpu.CompilerParams(dimension_semantics=("arbitrary",), vmem_limit_bytes=vmem),
        name="conv_block",
    )(x2d, x2d, norm_g3, mods, mods, mods, w_in, b_in[:, None, :], dw_w, dw_b[:, None, :], ln_g[:, None, :],
      ln_b[:, None, :], w_out, b_out[:, None, :])


def kernel(x, c, ada_w, ada_b, norm_g, ffn_w_in, ffn_w_out, gm_w_in, gm_ln_g, gm_ln_b, gm_ws, gm_bs, gm_w_out,
           cv_w_in, cv_b_in, cv_dw_w, cv_dw_b, cv_ln_g, cv_ln_b, cv_w_out, cv_b_out, final_g):
    bsz, seq, d = x.shape
    depth, n_sub = norm_g.shape[0], norm_g.shape[1]
    n_mixers = 2
    assert n_sub == 3 and ada_w.shape[2] == n_sub * N_MOD * d

    mods = _ada_table(c, ada_w, ada_b)
    norm_g3 = norm_g.reshape(depth * n_sub, 1, d)
    gm_w_in, gm_w_out = gm_w_in.astype(BF16), gm_w_out.astype(BF16)
    cv_w_in, cv_w_out = cv_w_in.astype(BF16), cv_w_out.astype(BF16)

    h = x.reshape(bsz * seq, d)
    for i in range(depth):
        h = _ffn(h, mods, norm_g3, ffn_w_in, ffn_w_out, i, 0, 0, seq)
        jm = i // n_mixers
        if i % n_mixers == 0:
            z = _gmlp_proj(h, mods, norm_g3, gm_w_in, i, jm, seq)
            h = _gmlp_mix(h, z, mods, gm_ln_g, gm_ln_b, gm_ws, gm_bs, gm_w_out, i, jm, seq)
        else:
            h = _conv_block(h, mods, norm_g3, cv_w_in, cv_b_in, cv_dw_w, cv_dw_b, cv_ln_g, cv_ln_b, cv_w_out,
                            cv_b_out, i, jm, seq)
        h = _ffn(h, mods, norm_g3, ffn_w_in, ffn_w_out, i, 2, 1, seq,
                 final_g=final_g if i == depth - 1 else None)
    return h.reshape(bsz, seq, d)
```

```python
import functools

import jax
import jax.numpy as jnp
from jax import lax
from jax.experimental import pallas as pl
from jax.experimental.pallas import tpu as pltpu

EPS = 1e-6
N_MOD = 3
CHUNK = 128
MOD_ROWS = 8
CONV_HALO = 32

V7X_VMEM_BYTES = 64 * 1024 * 1024
V7X_LANES = 128
SUBLANES = 8

F32 = jnp.float32
BF16 = jnp.bfloat16


def _vmem_limit(pipelined, single, scratch, temps):
    need = 2 * pipelined + single + scratch + temps
    cap = V7X_VMEM_BYTES - 1024 * 1024
    assert need <= cap, need
    return int(min(need + need // 10, cap))


def _nbytes(shape, dtype):
    n = 1
    for s in shape:
        n *= s
    return n * jnp.dtype(dtype).itemsize


def _dot(a, b):
    return jnp.dot(a, b, preferred_element_type=F32)


def _norm_mod_rows(x, g, shift, scale):
    ms = jnp.mean(x * x, axis=-1, keepdims=True)
    y = x * lax.rsqrt(ms + EPS) * g
    return y * (1.0 + scale) + shift


def _fill_h(h_ref, x_ref, g_ref, shift_ref, scale_ref, b, rows):
    tm = x_ref.shape[0]
    g = g_ref[...]
    shift = shift_ref[pl.ds(b, 1), :]
    scale = scale_ref[pl.ds(b, 1), :]

    def body(r, carry):
        rs = pl.ds(pl.multiple_of(r * rows, rows), rows)
        h_ref[rs, :] = _norm_mod_rows(x_ref[rs, :], g, shift, scale).astype(BF16)
        return carry

    lax.fori_loop(0, tm // rows, body, 0)


def _ada_kernel(c_ref, w_ref, b_ref, o_ref):
    c = c_ref[...]
    cond = (c * jax.nn.sigmoid(c)).astype(BF16)
    o_ref[...] = _dot(cond, w_ref[...].astype(BF16)) + b_ref[...]


def _ada_table(c, ada_w, ada_b, tn=1024):
    depth, d, n = ada_w.shape
    bsz = c.shape[0]
    assert bsz <= MOD_ROWS and n % d == 0 and d % tn == 0
    per = d // tn
    c8 = jnp.pad(c, ((0, MOD_ROWS - bsz), (0, 0)))
    b4 = ada_b.reshape(depth, n // d, 1, d)
    return pl.pallas_call(
        _ada_kernel,
        grid=(depth, n // tn),
        in_specs=[
            pl.BlockSpec((MOD_ROWS, d), lambda l, j: (0, 0)),
            pl.BlockSpec((None, d, tn), lambda l, j: (l, 0, j)),
            pl.BlockSpec((None, None, 1, tn), lambda l, j: (l, j // per, 0, j % per)),
        ],
        out_specs=pl.BlockSpec((None, None, MOD_ROWS, tn), lambda l, j: (l, j // per, 0, j % per)),
        out_shape=jax.ShapeDtypeStruct((depth, n // d, MOD_ROWS, d), F32),
        compiler_params=pltpu.CompilerParams(
            dimension_semantics=("parallel", "parallel"),
            vmem_limit_bytes=_vmem_limit(
                _nbytes((d, tn), F32) + _nbytes((MOD_ROWS, d + 2 * tn), F32), 0, 0,
                2 * _nbytes((d, tn), BF16) + 4 * _nbytes((MOD_ROWS, d), F32))),
        name="ada",
    )(c8, ada_w, b4)


def _mod_spec(layer, row, d):
    return pl.BlockSpec((None, None, MOD_ROWS, d), lambda *_: (layer, row, 0, 0))


def _ffn_kernel(x_ref, g_ref, shift_ref, scale_ref, gate_ref, wg_ref, wu_ref, wo_ref, *rest,
                blocks_per_batch, n_f, n_chunk, rows, final):
    if final:
        fg_ref, o_ref, h_ref = rest
    else:
        o_ref, h_ref = rest
    i = pl.program_id(0)
    f = pl.program_id(1)
    b = i // blocks_per_batch
    tm, d = o_ref.shape

    @pl.when(f == 0)
    def _():
        _fill_h(h_ref, x_ref, g_ref, shift_ref, scale_ref, b, rows)
        o_ref[...] = x_ref[...]

    half_gate = 0.5 * gate_ref[pl.ds(b, 1), :]
    h = h_ref[...]
    gt = _dot(h, wg_ref[...].astype(BF16))
    up = _dot(h, wu_ref[...].astype(BF16))
    a = (gt * jax.nn.sigmoid(gt) * up).astype(BF16)
    for n0 in range(0, d, n_chunk):
        ns = slice(n0, n0 + n_chunk)
        o_ref[:, ns] += half_gate[:, ns] * _dot(a, wo_ref[:, ns].astype(BF16))

    if final:
        @pl.when(f == n_f - 1)
        def _():
            fg = fg_ref[...]

            def body(r, carry):
                rs = pl.ds(pl.multiple_of(r * rows, rows), rows)
                o = o_ref[rs, :]
                ms = jnp.mean(o * o, axis=-1, keepdims=True)
                o_ref[rs, :] = o * lax.rsqrt(ms + EPS) * fg
                return carry

            lax.fori_loop(0, tm // rows, body, 0)


def _ffn(x2d, mods, norm_g3, w_in, w_out, layer, sub, which, seq, final_g=None, tm=1024, tf=512,
         n_chunk=512, rows=128):
    m, d = x2d.shape
    f_dim = w_out.shape[2]
    assert seq % tm == 0 and f_dim % tf == 0 and d % n_chunk == 0 and tm % rows == 0
    n_f = f_dim // tf
    row0 = sub * N_MOD
    in_specs = [
        pl.BlockSpec((tm, d), lambda i, f: (i, 0), pipeline_mode=pl.Buffered(1)),
        pl.BlockSpec((None, 1, d), lambda i, f: (layer * 3 + sub, 0, 0)),
        _mod_spec(layer, row0 + 0, d),
        _mod_spec(layer, row0 + 1, d),
        _mod_spec(layer, row0 + 2, d),
        pl.BlockSpec((None, None, d, tf), lambda i, f: (layer, which, 0, f)),
        pl.BlockSpec((None, None, d, tf), lambda i, f: (layer, which, 0, f + n_f)),
        pl.BlockSpec((None, None, tf, d), lambda i, f: (layer, which, f, 0)),
    ]
    args = [x2d, norm_g3, mods, mods, mods, w_in, w_in, w_out]
    if final_g is not None:
        in_specs.append(pl.BlockSpec((1, d), lambda i, f: (0, 0)))
        args.append(final_g.reshape(1, d))
    kern = functools.partial(_ffn_kernel, blocks_per_batch=seq // tm, n_f=n_f, n_chunk=n_chunk,
                             rows=rows, final=final_g is not None)
    vmem = _vmem_limit(
        pipelined=_nbytes((tm, d), F32) + 3 * _nbytes((d, tf), w_in.dtype) + 5 * _nbytes((MOD_ROWS, d), F32),
        single=_nbytes((tm, d), F32),
        scratch=_nbytes((tm, d), BF16),
        temps=2 * _nbytes((tm, tf), F32) + _nbytes((tm, tf), BF16))
    return pl.pallas_call(
        kern,
        grid=(m // tm, n_f),
        in_specs=in_specs,
        out_specs=pl.BlockSpec((tm, d), lambda i, f: (i, 0)),
        out_shape=jax.ShapeDtypeStruct((m, d), F32),
        scratch_shapes=[pltpu.VMEM((tm, d), BF16)],
        compiler_params=pltpu.CompilerParams(
            dimension_semantics=("parallel", "arbitrary"), vmem_limit_bytes=vmem),
        name="ffn",
    )(*args)


def _gmlp_proj_kernel(x_ref, g_ref, shift_ref, scale_ref, w_ref, z_ref, h_ref, *, blocks_per_batch, rows):
    i = pl.program_id(0)

    @pl.when(pl.program_id(1) == 0)
    def _():
        _fill_h(h_ref, x_ref, g_ref, shift_ref, scale_ref, i // blocks_per_batch, rows)

    z = _dot(h_ref[...], w_ref[...])
    z_ref[...] = 0.5 * z * (1.0 + lax.erf(z * (2.0 ** -0.5)))


def _gmlp_proj(x2d, mods, norm_g3, w_in, layer, jm, seq, tm=1024, tn=512, rows=128):
    m, d = x2d.shape
    n = w_in.shape[2]
    assert seq % tm == 0 and n % tn == 0
    kern = functools.partial(_gmlp_proj_kernel, blocks_per_batch=seq // tm, rows=rows)
    vmem = _vmem_limit(
        pipelined=_nbytes((tm, d), F32) + _nbytes((d, tn), BF16) + _nbytes((tm, tn), F32)
        + 3 * _nbytes((MOD_ROWS, d), F32),
        single=0, scratch=_nbytes((tm, d), BF16), temps=5 * _nbytes((tm, tn), F32))
    return pl.pallas_call(
        kern,
        grid=(m // tm, n // tn),
        in_specs=[
            pl.BlockSpec((tm, d), lambda i, j: (i, 0)),
            pl.BlockSpec((None, 1, d), lambda i, j: (layer * 3 + 1, 0, 0)),
            _mod_spec(layer, N_MOD + 0, d),
            _mod_spec(layer, N_MOD + 1, d),
            pl.BlockSpec((None, d, tn), lambda i, j: (jm, 0, j)),
        ],
        out_specs=pl.BlockSpec((tm, tn), lambda i, j: (i, j)),
        out_shape=jax.ShapeDtypeStruct((m, n), F32),
        scratch_shapes=[pltpu.VMEM((tm, d), BF16)],
        compiler_params=pltpu.CompilerParams(
            dimension_semantics=("parallel", "arbitrary"), vmem_limit_bytes=vmem),
        name="gmlp_proj",
    )(x2d, norm_g3, mods, mods, w_in)


def _layer_norm_rows(v, g, b):
    mu = jnp.mean(v, axis=-1, keepdims=True)
    dv = v - mu
    var = jnp.mean(dv * dv, axis=-1, keepdims=True)
    return dv * lax.rsqrt(var + EPS) * g + b


def _gmlp_mix_kernel(x_ref, u_ref, v_ref, lng_ref, lnb_ref, ws_ref, bs_ref, gate_ref, wo_ref, o_ref, s_ref,
                     *, blocks_per_batch, n_chunk):
    b = pl.program_id(0) // blocks_per_batch
    tm, e = s_ref.shape
    n_heads, length, _ = ws_ref.shape
    dh = e // n_heads
    lng = lng_ref[...]
    lnb = lnb_ref[...]
    causal = (lax.broadcasted_iota(jnp.int32, (length, length), 0)
              >= lax.broadcasted_iota(jnp.int32, (length, length), 1))

    def chunk(c, carry):
        rs = pl.ds(pl.multiple_of(c * length, length), length)
        vn = _layer_norm_rows(v_ref[rs, :], lng, lnb).astype(BF16)
        for hd in range(n_heads):
            cs = slice(hd * dh, (hd + 1) * dh)
            w_h = jnp.where(causal, ws_ref[hd], 0.0).astype(BF16)
            mixed = _dot(w_h, vn[:, cs]) + bs_ref[hd]
            s_ref[rs, cs] = (u_ref[rs, cs] * mixed).astype(BF16)
        return carry

    lax.fori_loop(0, tm // length, chunk, 0)
    gate = gate_ref[pl.ds(b, 1), :]
    s = s_ref[...]
    for n0 in range(0, o_ref.shape[1], n_chunk):
        ns = slice(n0, n0 + n_chunk)
        o_ref[:, ns] = x_ref[:, ns] + gate[:, ns] * _dot(s, wo_ref[:, ns])


def _gmlp_mix(x2d, z, mods, ln_g, ln_b, ws, bs, w_out, layer, jm, seq, tm=512, n_chunk=512):
    m, d = x2d.shape
    e = w_out.shape[1]
    n_heads, length = ws.shape[1], ws.shape[2]
    assert length == CHUNK and seq % tm == 0 and tm % length == 0 and (e // n_heads) % V7X_LANES == 0
    assert d % n_chunk == 0
    kern = functools.partial(_gmlp_mix_kernel, blocks_per_batch=seq // tm, n_chunk=n_chunk)
    vmem = _vmem_limit(
        pipelined=_nbytes((tm, d), F32) * 2 + 2 * _nbytes((tm, e), F32) + 2 * _nbytes((n_heads, length, length), F32)
        + 3 * _nbytes((MOD_ROWS, d), F32),
        single=_nbytes((e, d), BF16), scratch=_nbytes((tm, e), BF16),
        temps=3 * _nbytes((tm, n_chunk), F32) + 6 * _nbytes((length, e), F32))
    return pl.pallas_call(
        kern,
        grid=(m // tm,),
        in_specs=[
            pl.BlockSpec((tm, d), lambda i: (i, 0)),
            pl.BlockSpec((tm, e), lambda i: (i, 0)),
            pl.BlockSpec((tm, e), lambda i: (i, 1)),
            pl.BlockSpec((None, 1, e), lambda i: (jm, 0, 0)),
            pl.BlockSpec((None, 1, e), lambda i: (jm, 0, 0)),
            pl.BlockSpec((None, n_heads, length, length), lambda i: (jm, 0, 0, 0)),
            pl.BlockSpec((None, n_heads, length, 1), lambda i: (jm, 0, 0, 0)),
            _mod_spec(layer, N_MOD + 2, d),
            pl.BlockSpec((None, e, d), lambda i: (jm, 0, 0), pipeline_mode=pl.Buffered(1)),
        ],
        out_specs=pl.BlockSpec((tm, d), lambda i: (i, 0)),
        out_shape=jax.ShapeDtypeStruct((m, d), F32),
        scratch_shapes=[pltpu.VMEM((tm, e), BF16)],
        compiler_params=pltpu.CompilerParams(dimension_semantics=("parallel",), vmem_limit_bytes=vmem),
        name="gmlp_mix",
    )(x2d, z, z, ln_g[:, None, :], ln_b[:, None, :], ws, bs[..., None], mods, w_out)


def _conv_proj_kernel(x_ref, g_ref, shift_ref, scale_ref, wa_ref, wg_ref, ba_ref, bg_ref, y_ref, h_ref,
                      *, blocks_per_batch, rows):
    i = pl.program_id(0)

    @pl.when(pl.program_id(1) == 0)
    def _():
        _fill_h(h_ref, x_ref, g_ref, shift_ref, scale_ref, i // blocks_per_batch, rows)

    h = h_ref[...]
    a = _dot(h, wa_ref[...]) + ba_ref[...]
    gl = _dot(h, wg_ref[...]) + bg_ref[...]
    y_ref[...] = a * jax.nn.sigmoid(gl)


def _conv_proj(x2d, mods, norm_g3, w_in, b_in, layer, jm, seq, tm=1024, tn=512, rows=128):
    m, d = x2d.shape
    c = w_in.shape[2] // 2
    assert seq % tm == 0 and c % tn == 0
    n_j = c // tn
    kern = functools.partial(_conv_proj_kernel, blocks_per_batch=seq // tm, rows=rows)
    vmem = _vmem_limit(
        pipelined=_nbytes((tm, d), F32) + 2 * _nbytes((d, tn), BF16) + _nbytes((tm, tn), F32)
        + 3 * _nbytes((MOD_ROWS, d), F32) + 2 * _nbytes((MOD_ROWS, tn), F32),
        single=0, scratch=_nbytes((tm, d), BF16), temps=6 * _nbytes((tm, tn), F32))
    return pl.pallas_call(
        kern,
        grid=(m // tm, n_j),
        in_specs=[
            pl.BlockSpec((tm, d), lambda i, j: (i, 0)),
            pl.BlockSpec((None, 1, d), lambda i, j: (layer * 3 + 1, 0, 0)),
            _mod_spec(layer, N_MOD + 0, d),
            _mod_spec(layer, N_MOD + 1, d),
            pl.BlockSpec((None, d, tn), lambda i, j: (jm, 0, j)),
            pl.BlockSpec((None, d, tn), lambda i, j: (jm, 0, j + n_j)),
            pl.BlockSpec((None, 1, tn), lambda i, j: (jm, 0, j)),
            pl.BlockSpec((None, 1, tn), lambda i, j: (jm, 0, j + n_j)),
        ],
        out_specs=pl.BlockSpec((tm, tn), lambda i, j: (i, j)),
        out_shape=jax.ShapeDtypeStruct((m, c), F32),
        scratch_shapes=[pltpu.VMEM((tm, d), BF16)],
        compiler_params=pltpu.CompilerParams(
            dimension_semantics=("parallel", "arbitrary"), vmem_limit_bytes=vmem),
        name="conv_proj",
    )(x2d, norm_g3, mods, mods, w_in, w_in, b_in[:, None, :], b_in[:, None, :])


def _conv_mix_kernel(x_ref, y_ref, yprev_ref, dww_ref, dwb_ref, lng_ref, lnb_ref, gate_ref, wo_ref, bo_ref,
                     o_ref, win_ref, yc_ref, s_ref, *, blocks_per_batch, rows, n_chunk):
    i = pl.program_id(0)
    b = i // blocks_per_batch
    tm, c = y_ref.shape
    k_taps = dww_ref.shape[0]
    first = (i % blocks_per_batch) == 0

    win_ref[0:CONV_HALO, :] = jnp.where(first, 0.0, yprev_ref[...])
    win_ref[CONV_HALO:, :] = y_ref[...]

    lng = lng_ref[...]
    lnb = lnb_ref[...]
    base = CONV_HALO - (k_taps - 1)

    def chunk(r, carry):
        r0 = pl.multiple_of(r * rows, rows)
        for s in range(c // V7X_LANES):
            cs = slice(s * V7X_LANES, (s + 1) * V7X_LANES)
            acc = jnp.broadcast_to(dwb_ref[:, cs], (rows, V7X_LANES))
            for p in range(SUBLANES):
                part = None
                for a in range(CONV_HALO // SUBLANES + 1):
                    k = a * SUBLANES + p - base
                    if 0 <= k < k_taps:
                        n_load = rows + (SUBLANES if p else 0)
                        term = dww_ref[k:k + 1, cs] * win_ref[pl.ds(r0 + a * SUBLANES, n_load), cs]
                        part = term if part is None else part + term
                acc = acc + part[p:p + rows]
            yc_ref[pl.ds(r0, rows), cs] = acc
        yl = _layer_norm_rows(yc_ref[pl.ds(r0, rows), :], lng, lnb)
        s_ref[pl.ds(r0, rows), :] = (yl * jax.nn.sigmoid(yl)).astype(BF16)
        return carry

    lax.fori_loop(0, tm // rows, chunk, 0)
    gate = gate_ref[pl.ds(b, 1), :]
    s = s_ref[...]
    for n0 in range(0, o_ref.shape[1], n_chunk):
        ns = slice(n0, n0 + n_chunk)
        o_ref[:, ns] = x_ref[:, ns] + gate[:, ns] * (_dot(s, wo_ref[:, ns]) + bo_ref[:, ns])


def _conv_mix(x2d, y, mods, dw_w, dw_b, ln_g, ln_b, w_out, b_out, layer, jm, seq, tm=512, rows=64,
              n_chunk=512):
    m, d = x2d.shape
    c = y.shape[1]
    k_taps = dw_w.shape[1]
    assert k_taps - 1 <= CONV_HALO and seq % tm == 0 and tm % CONV_HALO == 0 and tm % rows == 0
    assert d % n_chunk == 0
    per = tm // CONV_HALO
    kern = functools.partial(_conv_mix_kernel, blocks_per_batch=seq // tm, rows=rows, n_chunk=n_chunk)
    vmem = _vmem_limit(
        pipelined=2 * _nbytes((tm, d), F32) + _nbytes((tm, c), F32) + _nbytes((CONV_HALO, c), F32)
        + _nbytes((CONV_HALO, c), F32) + 6 * _nbytes((MOD_ROWS, d), F32),
        single=_nbytes((c, d), BF16),
        scratch=_nbytes((tm + CONV_HALO, c), F32) + _nbytes((tm, c), F32) + _nbytes((tm, c), BF16),
        temps=4 * _nbytes((tm, n_chunk), F32) + 8 * _nbytes((rows, c), F32))
    return pl.pallas_call(
        kern,
        grid=(m // tm,),
        in_specs=[
            pl.BlockSpec((tm, d), lambda i: (i, 0)),
            pl.BlockSpec((tm, c), lambda i: (i, 0)),
            pl.BlockSpec((CONV_HALO, c), lambda i: (jnp.maximum(i * per - 1, 0), 0)),
            pl.BlockSpec((None, k_taps, c), lambda i: (jm, 0, 0)),
            pl.BlockSpec((None, 1, c), lambda i: (jm, 0, 0)),
            pl.BlockSpec((None, 1, c), lambda i: (jm, 0, 0)),
            pl.BlockSpec((None, 1, c), lambda i: (jm, 0, 0)),
            _mod_spec(layer, N_MOD + 2, d),
            pl.BlockSpec((None, c, d), lambda i: (jm, 0, 0), pipeline_mode=pl.Buffered(1)),
            pl.BlockSpec((None, 1, d), lambda i: (jm, 0, 0)),
        ],
        out_specs=pl.BlockSpec((tm, d), lambda i: (i, 0)),
        out_shape=jax.ShapeDtypeStruct((m, d), F32),
        scratch_shapes=[pltpu.VMEM((tm + CONV_HALO, c), F32), pltpu.VMEM((tm, c), F32), pltpu.VMEM((tm, c), BF16)],
        compiler_params=pltpu.CompilerParams(dimension_semantics=("parallel",), vmem_limit_bytes=vmem),
        name="conv_mix",
    )(x2d, y, y, dw_w, dw_b[:, None, :], ln_g[:, None, :], ln_b[:, None, :], mods, w_out, b_out[:, None, :])


def _conv_mixer_kernel(xc_ref, xr_ref, g_ref, shift_ref, scale_ref, gate_ref, wi_ref, bi_ref, dww_ref, dwb_ref,
                       lng_ref, lnb_ref, wo_ref, bo_ref, o_ref, h_ref, win_ref, s_ref, op_ref,
                       *, n_blocks, blocks_per_batch, rows):
    step = pl.program_id(0)
    _, n_slabs, _, lanes = win_ref.shape
    tm = s_ref.shape[0]
    n_tiles, _, tn = op_ref.shape
    yc_ref = win_ref.at[2]
    k_taps = dww_ref.shape[1]
    base = CONV_HALO - (k_taps - 1)
    slabs_per_tile = tn // lanes
    rd = step % 2

    @pl.when(step == 0)
    def _():
        win_ref[...] = jnp.zeros_like(win_ref)
        s_ref[...] = jnp.zeros_like(s_ref)

    b_in = jnp.minimum(step, n_blocks - 1) // blocks_per_batch
    h_ref[...] = _norm_mod_rows(xc_ref[...], g_ref[...], shift_ref[pl.ds(b_in, 1), :],
                                scale_ref[pl.ds(b_in, 1), :]).astype(BF16)

    def conv_slab(sl):
        for r0 in range(0, tm, rows):
            acc = jnp.broadcast_to(dwb_ref[sl], (rows, lanes))
            for p in range(SUBLANES):
                part = None
                for a8 in range(0, CONV_HALO + SUBLANES, SUBLANES):
                    k = a8 + p - base
                    if 0 <= k < k_taps:
                        n_load = rows + (SUBLANES if p else 0)
                        term = dww_ref[sl, k:k + 1, :] * win_ref[rd, sl, r0 + a8:r0 + a8 + n_load, :]
                        part = term if part is None else part + term
                acc = acc + part[p:p + rows]
            yc_ref[sl, r0:r0 + rows, :] = acc

    def tile(j, carry):
        for t in range(slabs_per_tile):
            conv_slab(j * slabs_per_tile + t)
        h = h_ref[...]
        op_ref[j] = _dot(s_ref[...], wo_ref[j])
        a = _dot(h, wi_ref[j]) + bi_ref[j]
        gl = _dot(h, wi_ref[n_tiles + j]) + bi_ref[n_tiles + j]
        y = a * jax.nn.sigmoid(gl)
        for t in range(slabs_per_tile):
            win_ref[1 - rd, j * slabs_per_tile + t, CONV_HALO:, :] = y[:, t * lanes:(t + 1) * lanes]
        return carry

    lax.fori_loop(0, n_tiles, tile, 0)

    blk_a = jnp.maximum(step - 2, 0)
    gate = gate_ref[pl.ds(blk_a // blocks_per_batch, 1), :]
    for j in range(n_tiles):
        ns = slice(j * tn, (j + 1) * tn)
        o_ref[:, ns] = xr_ref[:, ns] + gate[:, ns] * (op_ref[j] + bo_ref[:, ns])

    for r0 in range(0, tm, rows):
        tot = yc_ref[0, r0:r0 + rows, :]
        for sl in range(1, n_slabs):
            tot = tot + yc_ref[sl, r0:r0 + rows, :]
        mu = jnp.sum(tot, axis=-1, keepdims=True) * (1.0 / (n_slabs * lanes))
        sq = None
        for sl in range(n_slabs):
            dv = yc_ref[sl, r0:r0 + rows, :] - mu
            sq = dv * dv if sq is None else sq + dv * dv
        var = jnp.sum(sq, axis=-1, keepdims=True) * (1.0 / (n_slabs * lanes))
        rstd = lax.rsqrt(var + EPS)
        for sl in range(n_slabs):
            cs = slice(sl * lanes, (sl + 1) * lanes)
            yl = (yc_ref[sl, r0:r0 + rows, :] - mu) * rstd * lng_ref[:, cs] + lnb_ref[:, cs]
            s_ref[r0:r0 + rows, cs] = (yl * jax.nn.sigmoid(yl)).astype(BF16)

    for sl in range(n_slabs):
        tail = win_ref[rd, sl, tm:tm + CONV_HALO, :]
        win_ref[1 - rd, sl, 0:CONV_HALO, :] = jnp.where(step % blocks_per_batch == 0, 0.0, tail)


def _conv_mixer(x2d, mods, norm_g3, w_in, b_in, dw_w, dw_b, ln_g, ln_b, w_out, b_out, layer, jm, seq,
                tm=256, rows=64, tn=256):
    m, d = x2d.shape
    c = w_out.shape[0]
    k_taps = dw_w.shape[1]
    assert k_taps - 1 <= CONV_HALO and seq % tm == 0 and tm % rows == 0 and c % tn == 0 and d == c
    assert tn % V7X_LANES == 0
    n_blocks, n_tiles, n_slabs = m // tm, c // tn, c // V7X_LANES
    wi3 = w_in.reshape(d, 2 * n_tiles, tn).transpose(1, 0, 2)
    wo3 = w_out.reshape(c, n_tiles, tn).transpose(1, 0, 2)
    bi3 = b_in[jm].reshape(2 * n_tiles, 1, tn)
    dww3 = dw_w[jm].reshape(k_taps, n_slabs, V7X_LANES).transpose(1, 0, 2)
    dwb3 = dw_b[jm].reshape(n_slabs, 1, V7X_LANES)
    kern = functools.partial(_conv_mixer_kernel, n_blocks=n_blocks, blocks_per_batch=seq // tm, rows=rows)
    vmem = _vmem_limit(
        pipelined=3 * _nbytes((tm, d), F32) + 6 * _nbytes((MOD_ROWS, d), F32) + _nbytes((n_slabs, 32, V7X_LANES), F32)
        + _nbytes((2 * n_tiles, MOD_ROWS, tn), F32) + _nbytes((n_slabs, MOD_ROWS, V7X_LANES), F32),
        single=_nbytes((d, 2 * c), BF16) + _nbytes((c, d), BF16),
        scratch=_nbytes((tm, d), BF16) + _nbytes((2 * (tm + CONV_HALO), c), F32) + 2 * _nbytes((tm, c), F32)
        + _nbytes((tm, c), BF16),
        temps=4 * _nbytes((tm, d), F32))
    cur = lambda i: (jnp.minimum(i, n_blocks - 1), 0)
    lag2 = lambda i: (jnp.maximum(i - 2, 0), 0)
    whole = lambda i: (0, 0, 0)
    return pl.pallas_call(
        kern,
        grid=(n_blocks + 2,),
        in_specs=[
            pl.BlockSpec((tm, d), cur),
            pl.BlockSpec((tm, d), lag2),
            pl.BlockSpec((None, 1, d), lambda i: (layer * 3 + 1, 0, 0)),
            _mod_spec(layer, N_MOD + 0, d),
            _mod_spec(layer, N_MOD + 1, d),
            _mod_spec(layer, N_MOD + 2, d),
            pl.BlockSpec((2 * n_tiles, d, tn), whole, pipeline_mode=pl.Buffered(1)),
            pl.BlockSpec((2 * n_tiles, 1, tn), whole),
            pl.BlockSpec((n_slabs, k_taps, V7X_LANES), whole),
            pl.BlockSpec((n_slabs, 1, V7X_LANES), whole),
            pl.BlockSpec((None, 1, c), lambda i: (jm, 0, 0)),
            pl.BlockSpec((None, 1, c), lambda i: (jm, 0, 0)),
            pl.BlockSpec((n_tiles, c, tn), whole, pipeline_mode=pl.Buffered(1)),
            pl.BlockSpec((None, 1, d), lambda i: (jm, 0, 0)),
        ],
        out_specs=pl.BlockSpec((tm, d), lag2),
        out_shape=jax.ShapeDtypeStruct((m, d), F32),
        scratch_shapes=[pltpu.VMEM((tm, d), BF16),
                        pltpu.VMEM((3, n_slabs, tm + CONV_HALO, V7X_LANES), F32),
                        pltpu.VMEM((tm, c), BF16),
                        pltpu.VMEM((n_tiles, tm, tn), F32)],
        compiler_params=pltpu.CompilerParams(dimension_semantics=("arbitrary",), vmem_limit_bytes=vmem),
        name="conv_mixer",
    )(x2d, x2d, norm_g3, mods, mods, mods, wi3, bi3, dww3, dwb3, ln_g[:, None, :], ln_b[:, None, :], wo3,
      b_out[:, None, :])


def _conv_block_kernel(xc_ref, xr_ref, g_ref, shift_ref, scale_ref, gate_ref, wi_ref, bi_ref, dww_ref, dwb_ref,
                       lng_ref, lnb_ref, wo_ref, bo_ref, o_ref, win_ref, yc_ref, s_ref,
                       *, n_blocks, blocks_per_batch, rows, tn):
    step = pl.program_id(0)
    tm, c = yc_ref.shape
    d = o_ref.shape[1]
    k_taps = dww_ref.shape[0]
    base = CONV_HALO - (k_taps - 1)

    @pl.when(step == 0)
    def _():
        win_ref[0:CONV_HALO, :] = jnp.zeros((CONV_HALO, c), F32)
        s_ref[...] = jnp.zeros_like(s_ref)

    b_in = jnp.minimum(step, n_blocks - 1) // blocks_per_batch
    h = _norm_mod_rows(xc_ref[...], g_ref[...], shift_ref[pl.ds(b_in, 1), :],
                       scale_ref[pl.ds(b_in, 1), :]).astype(BF16)
    gate = gate_ref[pl.ds(jnp.maximum(step - 1, 0) // blocks_per_batch, 1), :]
    s_prev = s_ref[...]

    for j0 in range(0, c, tn):
        js = slice(j0, j0 + tn)
        a = _dot(h, wi_ref[:, js]) + bi_ref[:, js]
        gl = _dot(h, wi_ref[:, c + j0:c + j0 + tn]) + bi_ref[:, c + j0:c + j0 + tn]
        win_ref[CONV_HALO:, js] = a * jax.nn.sigmoid(gl)
        o_ref[:, js] = xr_ref[:, js] + gate[:, js] * (_dot(s_prev, wo_ref[:, js]) + bo_ref[:, js])
        for l0 in range(j0, j0 + tn, V7X_LANES):
            cs = slice(l0, l0 + V7X_LANES)
            for r0 in range(0, tm, rows):
                acc = jnp.broadcast_to(dwb_ref[:, cs], (rows, V7X_LANES))
                for p in range(SUBLANES):
                    part = None
                    for a8 in range(0, CONV_HALO + SUBLANES, SUBLANES):
                        k = a8 + p - base
                        if 0 <= k < k_taps:
                            n_load = rows + (SUBLANES if p else 0)
                            term = dww_ref[k:k + 1, cs] * win_ref[r0 + a8:r0 + a8 + n_load, cs]
                            part = term if part is None else part + term
                    acc = acc + part[p:p + rows]
                yc_ref[r0:r0 + rows, cs] = acc

    lng = lng_ref[...]
    lnb = lnb_ref[...]
    for r0 in range(0, tm, rows):
        yl = _layer_norm_rows(yc_ref[r0:r0 + rows, :], lng, lnb)
        s_ref[r0:r0 + rows, :] = (yl * jax.nn.sigmoid(yl)).astype(BF16)

    tail = win_ref[tm:tm + CONV_HALO, :]
    win_ref[0:CONV_HALO, :] = jnp.where((step + 1) % blocks_per_batch == 0, 0.0, tail)


def _conv_block(x2d, mods, norm_g3, w_in, b_in, dw_w, dw_b, ln_g, ln_b, w_out, b_out, layer, jm, seq,
                tm=256, rows=64, tn=256):
    m, d = x2d.shape
    c = w_out.shape[1]
    k_taps = dw_w.shape[1]
    assert k_taps - 1 <= CONV_HALO and seq % tm == 0 and tm % rows == 0 and c % tn == 0 and d == c
    n_blocks = m // tm
    kern = functools.partial(_conv_block_kernel, n_blocks=n_blocks, blocks_per_batch=seq // tm, rows=rows, tn=tn)
    vmem = _vmem_limit(
        pipelined=3 * _nbytes((tm, d), F32) + 8 * _nbytes((MOD_ROWS, d), F32) + _nbytes((CONV_HALO, c), F32)
        + _nbytes((MOD_ROWS, 2 * c), F32),
        single=_nbytes((d, 2 * c), BF16) + _nbytes((c, d), BF16),
        scratch=_nbytes((2 * tm + CONV_HALO, c), F32) + _nbytes((tm, c), BF16),
        temps=_nbytes((tm, d), BF16) + 3 * _nbytes((tm, d), F32))
    cur = lambda i: (jnp.minimum(i, n_blocks - 1), 0)
    lag = lambda i: (jnp.maximum(i - 1, 0), 0)
    return pl.pallas_call(
        kern,
        grid=(n_blocks + 1,),
        in_specs=[
            pl.BlockSpec((tm, d), cur),
            pl.BlockSpec((tm, d), lag),
            pl.BlockSpec((None, 1, d), lambda i: (layer * 3 + 1, 0, 0)),
            _mod_spec(layer, N_MOD + 0, d),
            _mod_spec(layer, N_MOD + 1, d),
            _mod_spec(layer, N_MOD + 2, d),
            pl.BlockSpec((None, d, 2 * c), lambda i: (jm, 0, 0), pipeline_mode=pl.Buffered(1)),
            pl.BlockSpec((None, 1, 2 * c), lambda i: (jm, 0, 0)),
            pl.BlockSpec((None, k_taps, c), lambda i: (jm, 0, 0)),
            pl.BlockSpec((None, 1, c), lambda i: (jm, 0, 0)),
            pl.BlockSpec((None, 1, c), lambda i: (jm, 0, 0)),
            pl.BlockSpec((None, 1, c), lambda i: (jm, 0, 0)),
            pl.BlockSpec((None, c, d), lambda i: (jm, 0, 0), pipeline_mode=pl.Buffered(1)),
            pl.BlockSpec((None, 1, d), lambda i: (jm, 0, 0)),
        ],
        out_specs=pl.BlockSpec((tm, d), lag),
        out_shape=jax.ShapeDtypeStruct((m, d), F32),
        scratch_shapes=[pltpu.VMEM((tm + CONV_HALO, c), F32), pltpu.VMEM((tm, c), F32), pltpu.VMEM((tm, c), BF16)],
        compiler_params=pltpu.CompilerParams(dimension_semantics=("arbitrary",), vmem_limit_bytes=vmem),
        name="conv_block",
    )(x2d, x2d, norm_g3, mods, mods, mods, w_in, b_in[:, None, :], dw_w, dw_b[:, None, :], ln_g[:, None, :],
      ln_b[:, None, :], w_out, b_out[:, None, :])


def _gelu(z):
    return 0.5 * z * (1.0 + lax.erf(z * (2.0 ** -0.5)))


def _gmlp_block_kernel(xc_ref, xr_ref, g_ref, shift_ref, scale_ref, gate_ref, wi_ref, lng_ref, lnb_ref, ws_ref,
                       bs_ref, wo_ref, o_ref, u_ref, v_ref, vn_ref, s_ref, wsm_ref,
                       *, n_blocks, blocks_per_batch, rows, tn):
    step = pl.program_id(0)
    tm, e = u_ref.shape
    n_heads, length, _ = ws_ref.shape
    dh = e // n_heads

    @pl.when(step == 0)
    def _():
        s_ref[...] = jnp.zeros_like(s_ref)
        causal = (lax.broadcasted_iota(jnp.int32, (length, length), 0)
                  >= lax.broadcasted_iota(jnp.int32, (length, length), 1))
        for hd in range(n_heads):
            wsm_ref[hd] = jnp.where(causal, ws_ref[hd], 0.0).astype(BF16)

    b_in = jnp.minimum(step, n_blocks - 1) // blocks_per_batch
    h = _norm_mod_rows(xc_ref[...], g_ref[...], shift_ref[pl.ds(b_in, 1), :],
                       scale_ref[pl.ds(b_in, 1), :]).astype(BF16)
    gate = gate_ref[pl.ds(jnp.maximum(step - 1, 0) // blocks_per_batch, 1), :]
    s_prev = s_ref[...]

    for j0 in range(0, e, tn):
        js = slice(j0, j0 + tn)
        v_ref[:, js] = _gelu(_dot(h, wi_ref[:, e + j0:e + j0 + tn]))
        o_ref[:, js] = xr_ref[:, js] + gate[:, js] * _dot(s_prev, wo_ref[:, js])

    lng = lng_ref[...]
    lnb = lnb_ref[...]
    for r0 in range(0, tm, rows):
        vn_ref[r0:r0 + rows, :] = _layer_norm_rows(v_ref[r0:r0 + rows, :], lng, lnb).astype(BF16)

    for j0 in range(0, e, tn):
        u_ref[:, j0:j0 + tn] = _gelu(_dot(h, wi_ref[:, j0:j0 + tn]))

    for hd in range(n_heads):
        cs = slice(hd * dh, (hd + 1) * dh)
        bias = bs_ref[hd]
        for r0 in range(0, tm, length):
            rs = slice(r0, r0 + length)
            mixed = _dot(wsm_ref[hd], vn_ref[rs, cs]) + bias
            s_ref[rs, cs] = (u_ref[rs, cs] * mixed).astype(BF16)


def _gmlp_block(x2d, mods, norm_g3, w_in, ln_g, ln_b, ws, bs, w_out, layer, jm, seq, tm=256, rows=64, tn=256):
    m, d = x2d.shape
    e = w_out.shape[1]
    n_heads, length = ws.shape[1], ws.shape[2]
    assert length == CHUNK and seq % tm == 0 and tm % length == 0 and tm % rows == 0 and e % tn == 0
    assert (e // n_heads) % V7X_LANES == 0 and d == e
    n_blocks = m // tm
    kern = functools.partial(_gmlp_block_kernel, n_blocks=n_blocks, blocks_per_batch=seq // tm, rows=rows, tn=tn)
    vmem = _vmem_limit(
        pipelined=3 * _nbytes((tm, d), F32) + 6 * _nbytes((MOD_ROWS, d), F32)
        + 2 * _nbytes((n_heads, length, length), F32),
        single=_nbytes((d, 2 * e), BF16) + _nbytes((e, d), BF16),
        scratch=2 * _nbytes((tm, e), F32) + 2 * _nbytes((tm, e), BF16) + _nbytes((n_heads, length, length), BF16),
        temps=_nbytes((tm, d), BF16) + 3 * _nbytes((tm, d), F32))
    cur = lambda i: (jnp.minimum(i, n_blocks - 1), 0)
    lag = lambda i: (jnp.maximum(i - 1, 0), 0)
    return pl.pallas_call(
        kern,
        grid=(n_blocks + 1,),
        in_specs=[
            pl.BlockSpec((tm, d), cur),
            pl.BlockSpec((tm, d), lag),
            pl.BlockSpec((None, 1, d), lambda i: (layer * 3 + 1, 0, 0)),
            _mod_spec(layer, N_MOD + 0, d),
            _mod_spec(layer, N_MOD + 1, d),
            _mod_spec(layer, N_MOD + 2, d),
            pl.BlockSpec((None, d, 2 * e), lambda i: (jm, 0, 0), pipeline_mode=pl.Buffered(1)),
            pl.BlockSpec((None, 1, e), lambda i: (jm, 0, 0)),
            pl.BlockSpec((None, 1, e), lambda i: (jm, 0, 0)),
            pl.BlockSpec((None, n_heads, length, length), lambda i: (jm, 0, 0, 0)),
            pl.BlockSpec((None, n_heads, length, 1), lambda i: (jm, 0, 0, 0)),
            pl.BlockSpec((None, e, d), lambda i: (jm, 0, 0), pipeline_mode=pl.Buffered(1)),
        ],
        out_specs=pl.BlockSpec((tm, d), lag),
        out_shape=jax.ShapeDtypeStruct((m, d), F32),
        scratch_shapes=[pltpu.VMEM((tm, e), F32), pltpu.VMEM((tm, e), F32), pltpu.VMEM((tm, e), BF16),
                        pltpu.VMEM((tm, e), BF16), pltpu.VMEM((n_heads, length, length), BF16)],
        compiler_params=pltpu.CompilerParams(dimension_semantics=("arbitrary",), vmem_limit_bytes=vmem),
        name="gmlp_block",
    )(x2d, x2d, norm_g3, mods, mods, mods, w_in, ln_g[:, None, :], ln_b[:, None, :], ws, bs[..., None], w_out)


def kernel(x, c, ada_w, ada_b, norm_g, ffn_w_in, ffn_w_out, gm_w_in, gm_ln_g, gm_ln_b, gm_ws, gm_bs, gm_w_out,
           cv_w_in, cv_b_in, cv_dw_w, cv_dw_b, cv_ln_g, cv_ln_b, cv_w_out, cv_b_out, final_g):
    bsz, seq, d = x.shape
    depth, n_sub = norm_g.shape[0], norm_g.shape[1]
    n_mixers = 2
    assert n_sub == 3 and ada_w.shape[2] == n_sub * N_MOD * d

    mods = _ada_table(c, ada_w, ada_b)
    norm_g3 = norm_g.reshape(depth * n_sub, 1, d)
    gm_w_in, gm_w_out = gm_w_in.astype(BF16), gm_w_out.astype(BF16)
    cv_w_in, cv_w_out = cv_w_in.astype(BF16), cv_w_out.astype(BF16)

    h = x.reshape(bsz * seq, d)
    for i in range(depth):
        h = _ffn(h, mods, norm_g3, ffn_w_in, ffn_w_out, i, 0, 0, seq)
        jm = i // n_mixers
        if i % n_mixers == 0:
            h = _gmlp_block(h, mods, norm_g3, gm_w_in, gm_ln_g, gm_ln_b, gm_ws, gm_bs, gm_w_out, i, jm, seq)
        else:
            h = _conv_block(h, mods, norm_g3, cv_w_in, cv_b_in, cv_dw_w, cv_dw_b, cv_ln_g, cv_ln_b, cv_w_out,
                            cv_b_out, i, jm, seq)
        h = _ffn(h, mods, norm_g3, ffn_w_in, ffn_w_out, i, 2, 1, seq,
                 final_g=final_g if i == depth - 1 else None)
    return h.reshape(bsz, seq, d)
```

```python
import functools

import jax
import jax.numpy as jnp
from jax import lax
from jax.experimental import pallas as pl
from jax.experimental.pallas import tpu as pltpu

EPS = 1e-6
N_MOD = 3
CHUNK = 128
MOD_ROWS = 8
CONV_HALO = 32

V7X_VMEM_BYTES = 64 * 1024 * 1024
V7X_LANES = 128
SUBLANES = 8

F32 = jnp.float32
BF16 = jnp.bfloat16


def _vmem_limit(pipelined, single, scratch, temps):
    need = 2 * pipelined + single + scratch + temps
    cap = V7X_VMEM_BYTES - 1024 * 1024
    assert need <= cap, need
    return int(min(need + need // 10, cap))


def _nbytes(shape, dtype):
    n = 1
    for s in shape:
        n *= s
    return n * jnp.dtype(dtype).itemsize


def _dot(a, b):
    return jnp.dot(a, b, preferred_element_type=F32)


def _norm_mod_rows(x, g, shift, scale):
    ms = jnp.mean(x * x, axis=-1, keepdims=True)
    return x * lax.rsqrt(ms + EPS) * (g * (1.0 + scale)) + shift


def _fill_h(h_ref, x_ref, g_ref, shift_ref, scale_ref, b, rows):
    tm = x_ref.shape[0]
    g = g_ref[...]
    shift = shift_ref[pl.ds(b, 1), :]
    scale = scale_ref[pl.ds(b, 1), :]

    def body(r, carry):
        rs = pl.ds(pl.multiple_of(r * rows, rows), rows)
        h_ref[rs, :] = _norm_mod_rows(x_ref[rs, :], g, shift, scale).astype(BF16)
        return carry

    lax.fori_loop(0, tm // rows, body, 0)


def _ada_kernel(c_ref, w_ref, b_ref, o_ref):
    c = c_ref[...]
    cond = (c * jax.nn.sigmoid(c)).astype(BF16)
    o_ref[...] = _dot(cond, w_ref[...].astype(BF16)) + b_ref[...]


def _ada_table(c, ada_w, ada_b, tn=1024):
    depth, d, n = ada_w.shape
    bsz = c.shape[0]
    assert bsz <= MOD_ROWS and n % d == 0 and d % tn == 0
    per = d // tn
    c8 = jnp.pad(c, ((0, MOD_ROWS - bsz), (0, 0)))
    b4 = ada_b.reshape(depth, n // d, 1, d)
    return pl.pallas_call(
        _ada_kernel,
        grid=(depth, n // tn),
        in_specs=[
            pl.BlockSpec((MOD_ROWS, d), lambda l, j: (0, 0)),
            pl.BlockSpec((None, d, tn), lambda l, j: (l, 0, j)),
            pl.BlockSpec((None, None, 1, tn), lambda l, j: (l, j // per, 0, j % per)),
        ],
        out_specs=pl.BlockSpec((None, None, MOD_ROWS, tn), lambda l, j: (l, j // per, 0, j % per)),
        out_shape=jax.ShapeDtypeStruct((depth, n // d, MOD_ROWS, d), F32),
        compiler_params=pltpu.CompilerParams(
            dimension_semantics=("parallel", "parallel"),
            vmem_limit_bytes=_vmem_limit(
                _nbytes((d, tn), F32) + _nbytes((MOD_ROWS, d + 2 * tn), F32), 0, 0,
                2 * _nbytes((d, tn), BF16) + 4 * _nbytes((MOD_ROWS, d), F32))),
        name="ada",
    )(c8, ada_w, b4)


def _mod_spec(layer, row, d):
    return pl.BlockSpec((None, None, MOD_ROWS, d), lambda *_: (layer, row, 0, 0))


def _ffn_kernel(x_hbm, g_ref, shift_ref, scale_ref, gate_ref, wg_ref, wu_ref, wo_ref, *rest,
                blocks_per_batch, n_i, n_f, n_chunk, rows, final):
    if final:
        fg_ref, o_ref, h_ref, xbuf_ref, sem = rest
    else:
        o_ref, h_ref, xbuf_ref, sem = rest
    i = pl.program_id(0)
    f = pl.program_id(1)
    b = i // blocks_per_batch
    tm, d = o_ref.shape
    n_r = tm // rows

    def x_copy(blk, r, slot):
        return pltpu.make_async_copy(x_hbm.at[pl.ds(blk * tm + r * rows, rows), :], xbuf_ref.at[slot],
                                     sem.at[slot])

    @pl.when(f == 0)
    def _():
        @pl.when(i == 0)
        def _():
            x_copy(i, 0, 0).start()

        g = g_ref[...]
        shift = shift_ref[pl.ds(b, 1), :]
        scale = scale_ref[pl.ds(b, 1), :]
        for r in range(n_r):
            slot = r % 2
            x_copy(i, r, slot).wait()
            if r + 1 < n_r:
                x_copy(i, r + 1, 1 - slot).start()
            x = xbuf_ref[slot]
            rs = slice(r * rows, (r + 1) * rows)
            h_ref[rs, :] = _norm_mod_rows(x, g, shift, scale).astype(BF16)
            o_ref[rs, :] = x

    @pl.when((f == n_f - 1) & (i + 1 < n_i))
    def _():
        x_copy(i + 1, 0, 0).start()

    half_gate = 0.5 * gate_ref[pl.ds(b, 1), :]
    h = h_ref[...]
    gt = _dot(h, wg_ref[...].astype(BF16))
    up = _dot(h, wu_ref[...].astype(BF16))
    a = (gt * jax.nn.sigmoid(gt) * up).astype(BF16)
    for n0 in range(0, d, n_chunk):
        ns = slice(n0, n0 + n_chunk)
        o_ref[:, ns] += half_gate[:, ns] * _dot(a, wo_ref[:, ns].astype(BF16))

    if final:
        @pl.when(f == n_f - 1)
        def _():
            fg = fg_ref[...]

            def body(r, carry):
                rs = pl.ds(pl.multiple_of(r * rows, rows), rows)
                o = o_ref[rs, :]
                ms = jnp.mean(o * o, axis=-1, keepdims=True)
                o_ref[rs, :] = o * lax.rsqrt(ms + EPS) * fg
                return carry

            lax.fori_loop(0, tm // rows, body, 0)


def _ffn(x2d, mods, norm_g3, w_in, w_out, layer, sub, which, seq, final_g=None, tm=1024, tf=512,
         n_chunk=512, rows=128):
    m, d = x2d.shape
    f_dim = w_out.shape[2]
    assert seq % tm == 0 and f_dim % tf == 0 and d % n_chunk == 0 and tm % (2 * rows) == 0
    n_f = f_dim // tf
    row0 = sub * N_MOD
    in_specs = [
        pl.BlockSpec(memory_space=pl.ANY),
        pl.BlockSpec((None, 1, d), lambda i, f: (layer * 3 + sub, 0, 0)),
        _mod_spec(layer, row0 + 0, d),
        _mod_spec(layer, row0 + 1, d),
        _mod_spec(layer, row0 + 2, d),
        pl.BlockSpec((None, None, d, tf), lambda i, f: (layer, which, 0, f)),
        pl.BlockSpec((None, None, d, tf), lambda i, f: (layer, which, 0, f + n_f)),
        pl.BlockSpec((None, None, tf, d), lambda i, f: (layer, which, f, 0)),
    ]
    args = [x2d, norm_g3, mods, mods, mods, w_in, w_in, w_out]
    if final_g is not None:
        in_specs.append(pl.BlockSpec((1, d), lambda i, f: (0, 0)))
        args.append(final_g.reshape(1, d))
    kern = functools.partial(_ffn_kernel, blocks_per_batch=seq // tm, n_i=m // tm, n_f=n_f, n_chunk=n_chunk,
                             rows=rows, final=final_g is not None)
    vmem = _vmem_limit(
        pipelined=_nbytes((tm, d), F32) + 3 * _nbytes((d, tf), w_in.dtype) + 5 * _nbytes((MOD_ROWS, d), F32),
        single=0,
        scratch=_nbytes((tm, d), BF16) + _nbytes((2, rows, d), F32),
        temps=2 * _nbytes((tm, tf), F32) + _nbytes((tm, tf), BF16))
    return pl.pallas_call(
        kern,
        grid=(m // tm, n_f),
        in_specs=in_specs,
        out_specs=pl.BlockSpec((tm, d), lambda i, f: (i, 0)),
        out_shape=jax.ShapeDtypeStruct((m, d), F32),
        scratch_shapes=[pltpu.VMEM((tm, d), BF16), pltpu.VMEM((2, rows, d), F32), pltpu.SemaphoreType.DMA((2,))],
        compiler_params=pltpu.CompilerParams(
            dimension_semantics=("arbitrary", "arbitrary"), vmem_limit_bytes=vmem),
        name="ffn",
    )(*args)


def _gmlp_proj_kernel(x_ref, g_ref, shift_ref, scale_ref, w_ref, z_ref, h_ref, *, blocks_per_batch, rows):
    i = pl.program_id(0)

    @pl.when(pl.program_id(1) == 0)
    def _():
        _fill_h(h_ref, x_ref, g_ref, shift_ref, scale_ref, i // blocks_per_batch, rows)

    z = _dot(h_ref[...], w_ref[...])
    z_ref[...] = 0.5 * z * (1.0 + lax.erf(z * (2.0 ** -0.5)))


def _gmlp_proj(x2d, mods, norm_g3, w_in, layer, jm, seq, tm=1024, tn=512, rows=128):
    m, d = x2d.shape
    n = w_in.shape[2]
    assert seq % tm == 0 and n % tn == 0
    kern = functools.partial(_gmlp_proj_kernel, blocks_per_batch=seq // tm, rows=rows)
    vmem = _vmem_limit(
        pipelined=_nbytes((tm, d), F32) + _nbytes((d, tn), BF16) + _nbytes((tm, tn), F32)
        + 3 * _nbytes((MOD_ROWS, d), F32),
        single=0, scratch=_nbytes((tm, d), BF16), temps=5 * _nbytes((tm, tn), F32))
    return pl.pallas_call(
        kern,
        grid=(m // tm, n // tn),
        in_specs=[
            pl.BlockSpec((tm, d), lambda i, j: (i, 0)),
            pl.BlockSpec((None, 1, d), lambda i, j: (layer * 3 + 1, 0, 0)),
            _mod_spec(layer, N_MOD + 0, d),
            _mod_spec(layer, N_MOD + 1, d),
            pl.BlockSpec((None, d, tn), lambda i, j: (jm, 0, j)),
        ],
        out_specs=pl.BlockSpec((tm, tn), lambda i, j: (i, j)),
        out_shape=jax.ShapeDtypeStruct((m, n), F32),
        scratch_shapes=[pltpu.VMEM((tm, d), BF16)],
        compiler_params=pltpu.CompilerParams(
            dimension_semantics=("parallel", "arbitrary"), vmem_limit_bytes=vmem),
        name="gmlp_proj",
    )(x2d, norm_g3, mods, mods, w_in)


def _layer_norm_rows(v, g, b):
    mu = jnp.mean(v, axis=-1, keepdims=True)
    dv = v - mu
    var = jnp.mean(dv * dv, axis=-1, keepdims=True)
    return dv * lax.rsqrt(var + EPS) * g + b


def _gmlp_mix_kernel(x_ref, u_ref, v_ref, lng_ref, lnb_ref, ws_ref, bs_ref, gate_ref, wo_ref, o_ref, s_ref,
                     *, blocks_per_batch, n_chunk):
    b = pl.program_id(0) // blocks_per_batch
    tm, e = s_ref.shape
    n_heads, length, _ = ws_ref.shape
    dh = e // n_heads
    lng = lng_ref[...]
    lnb = lnb_ref[...]
    causal = (lax.broadcasted_iota(jnp.int32, (length, length), 0)
              >= lax.broadcasted_iota(jnp.int32, (length, length), 1))

    def chunk(c, carry):
        rs = pl.ds(pl.multiple_of(c * length, length), length)
        vn = _layer_norm_rows(v_ref[rs, :], lng, lnb).astype(BF16)
        for hd in range(n_heads):
            cs = slice(hd * dh, (hd + 1) * dh)
            w_h = jnp.where(causal, ws_ref[hd], 0.0).astype(BF16)
            mixed = _dot(w_h, vn[:, cs]) + bs_ref[hd]
            s_ref[rs, cs] = (u_ref[rs, cs] * mixed).astype(BF16)
        return carry

    lax.fori_loop(0, tm // length, chunk, 0)
    gate = gate_ref[pl.ds(b, 1), :]
    s = s_ref[...]
    for n0 in range(0, o_ref.shape[1], n_chunk):
        ns = slice(n0, n0 + n_chunk)
        o_ref[:, ns] = x_ref[:, ns] + gate[:, ns] * _dot(s, wo_ref[:, ns])


def _gmlp_mix(x2d, z, mods, ln_g, ln_b, ws, bs, w_out, layer, jm, seq, tm=512, n_chunk=512):
    m, d = x2d.shape
    e = w_out.shape[1]
    n_heads, length = ws.shape[1], ws.shape[2]
    assert length == CHUNK and seq % tm == 0 and tm % length == 0 and (e // n_heads) % V7X_LANES == 0
    assert d % n_chunk == 0
    kern = functools.partial(_gmlp_mix_kernel, blocks_per_batch=seq // tm, n_chunk=n_chunk)
    vmem = _vmem_limit(
        pipelined=_nbytes((tm, d), F32) * 2 + 2 * _nbytes((tm, e), F32) + 2 * _nbytes((n_heads, length, length), F32)
        + 3 * _nbytes((MOD_ROWS, d), F32),
        single=_nbytes((e, d), BF16), scratch=_nbytes((tm, e), BF16),
        temps=3 * _nbytes((tm, n_chunk), F32) + 6 * _nbytes((length, e), F32))
    return pl.pallas_call(
        kern,
        grid=(m // tm,),
        in_specs=[
            pl.BlockSpec((tm, d), lambda i: (i, 0)),
            pl.BlockSpec((tm, e), lambda i: (i, 0)),
            pl.BlockSpec((tm, e), lambda i: (i, 1)),
            pl.BlockSpec((None, 1, e), lambda i: (jm, 0, 0)),
            pl.BlockSpec((None, 1, e), lambda i: (jm, 0, 0)),
            pl.BlockSpec((None, n_heads, length, length), lambda i: (jm, 0, 0, 0)),
            pl.BlockSpec((None, n_heads, length, 1), lambda i: (jm, 0, 0, 0)),
            _mod_spec(layer, N_MOD + 2, d),
            pl.BlockSpec((None, e, d), lambda i: (jm, 0, 0), pipeline_mode=pl.Buffered(1)),
        ],
        out_specs=pl.BlockSpec((tm, d), lambda i: (i, 0)),
        out_shape=jax.ShapeDtypeStruct((m, d), F32),
        scratch_shapes=[pltpu.VMEM((tm, e), BF16)],
        compiler_params=pltpu.CompilerParams(dimension_semantics=("parallel",), vmem_limit_bytes=vmem),
        name="gmlp_mix",
    )(x2d, z, z, ln_g[:, None, :], ln_b[:, None, :], ws, bs[..., None], mods, w_out)


def _conv_proj_kernel(x_ref, g_ref, shift_ref, scale_ref, wa_ref, wg_ref, ba_ref, bg_ref, y_ref, h_ref,
                      *, blocks_per_batch, rows):
    i = pl.program_id(0)

    @pl.when(pl.program_id(1) == 0)
    def _():
        _fill_h(h_ref, x_ref, g_ref, shift_ref, scale_ref, i // blocks_per_batch, rows)

    h = h_ref[...]
    a = _dot(h, wa_ref[...]) + ba_ref[...]
    gl = _dot(h, wg_ref[...]) + bg_ref[...]
    y_ref[...] = a * jax.nn.sigmoid(gl)


def _conv_proj(x2d, mods, norm_g3, w_in, b_in, layer, jm, seq, tm=1024, tn=512, rows=128):
    m, d = x2d.shape
    c = w_in.shape[2] // 2
    assert seq % tm == 0 and c % tn == 0
    n_j = c // tn
    kern = functools.partial(_conv_proj_kernel, blocks_per_batch=seq // tm, rows=rows)
    vmem = _vmem_limit(
        pipelined=_nbytes((tm, d), F32) + 2 * _nbytes((d, tn), BF16) + _nbytes((tm, tn), F32)
        + 3 * _nbytes((MOD_ROWS, d), F32) + 2 * _nbytes((MOD_ROWS, tn), F32),
        single=0, scratch=_nbytes((tm, d), BF16), temps=6 * _nbytes((tm, tn), F32))
    return pl.pallas_call(
        kern,
        grid=(m // tm, n_j),
        in_specs=[
            pl.BlockSpec((tm, d), lambda i, j: (i, 0)),
            pl.BlockSpec((None, 1, d), lambda i, j: (layer * 3 + 1, 0, 0)),
            _mod_spec(layer, N_MOD + 0, d),
            _mod_spec(layer, N_MOD + 1, d),
            pl.BlockSpec((None, d, tn), lambda i, j: (jm, 0, j)),
            pl.BlockSpec((None, d, tn), lambda i, j: (jm, 0, j + n_j)),
            pl.BlockSpec((None, 1, tn), lambda i, j: (jm, 0, j)),
            pl.BlockSpec((None, 1, tn), lambda i, j: (jm, 0, j + n_j)),
        ],
        out_specs=pl.BlockSpec((tm, tn), lambda i, j: (i, j)),
        out_shape=jax.ShapeDtypeStruct((m, c), F32),
        scratch_shapes=[pltpu.VMEM((tm, d), BF16)],
        compiler_params=pltpu.CompilerParams(
            dimension_semantics=("parallel", "arbitrary"), vmem_limit_bytes=vmem),
        name="conv_proj",
    )(x2d, norm_g3, mods, mods, w_in, w_in, b_in[:, None, :], b_in[:, None, :])


def _conv_mix_kernel(x_ref, y_ref, yprev_ref, dww_ref, dwb_ref, lng_ref, lnb_ref, gate_ref, wo_ref, bo_ref,
                     o_ref, win_ref, yc_ref, s_ref, *, blocks_per_batch, rows, n_chunk):
    i = pl.program_id(0)
    b = i // blocks_per_batch
    tm, c = y_ref.shape
    k_taps = dww_ref.shape[0]
    first = (i % blocks_per_batch) == 0

    win_ref[0:CONV_HALO, :] = jnp.where(first, 0.0, yprev_ref[...])
    win_ref[CONV_HALO:, :] = y_ref[...]

    lng = lng_ref[...]
    lnb = lnb_ref[...]
    base = CONV_HALO - (k_taps - 1)

    def chunk(r, carry):
        r0 = pl.multiple_of(r * rows, rows)
        for s in range(c // V7X_LANES):
            cs = slice(s * V7X_LANES, (s + 1) * V7X_LANES)
            acc = jnp.broadcast_to(dwb_ref[:, cs], (rows, V7X_LANES))
            for p in range(SUBLANES):
                part = None
                for a in range(CONV_HALO // SUBLANES + 1):
                    k = a * SUBLANES + p - base
                    if 0 <= k < k_taps:
                        n_load = rows + (SUBLANES if p else 0)
                        term = dww_ref[k:k + 1, cs] * win_ref[pl.ds(r0 + a * SUBLANES, n_load), cs]
                        part = term if part is None else part + term
                acc = acc + part[p:p + rows]
            yc_ref[pl.ds(r0, rows), cs] = acc
        yl = _layer_norm_rows(yc_ref[pl.ds(r0, rows), :], lng, lnb)
        s_ref[pl.ds(r0, rows), :] = (yl * jax.nn.sigmoid(yl)).astype(BF16)
        return carry

    lax.fori_loop(0, tm // rows, chunk, 0)
    gate = gate_ref[pl.ds(b, 1), :]
    s = s_ref[...]
    for n0 in range(0, o_ref.shape[1], n_chunk):
        ns = slice(n0, n0 + n_chunk)
        o_ref[:, ns] = x_ref[:, ns] + gate[:, ns] * (_dot(s, wo_ref[:, ns]) + bo_ref[:, ns])


def _conv_mix(x2d, y, mods, dw_w, dw_b, ln_g, ln_b, w_out, b_out, layer, jm, seq, tm=512, rows=64,
              n_chunk=512):
    m, d = x2d.shape
    c = y.shape[1]
    k_taps = dw_w.shape[1]
    assert k_taps - 1 <= CONV_HALO and seq % tm == 0 and tm % CONV_HALO == 0 and tm % rows == 0
    assert d % n_chunk == 0
    per = tm // CONV_HALO
    kern = functools.partial(_conv_mix_kernel, blocks_per_batch=seq // tm, rows=rows, n_chunk=n_chunk)
    vmem = _vmem_limit(
        pipelined=2 * _nbytes((tm, d), F32) + _nbytes((tm, c), F32) + _nbytes((CONV_HALO, c), F32)
        + _nbytes((CONV_HALO, c), F32) + 6 * _nbytes((MOD_ROWS, d), F32),
        single=_nbytes((c, d), BF16),
        scratch=_nbytes((tm + CONV_HALO, c), F32) + _nbytes((tm, c), F32) + _nbytes((tm, c), BF16),
        temps=4 * _nbytes((tm, n_chunk), F32) + 8 * _nbytes((rows, c), F32))
    return pl.pallas_call(
        kern,
        grid=(m // tm,),
        in_specs=[
            pl.BlockSpec((tm, d), lambda i: (i, 0)),
            pl.BlockSpec((tm, c), lambda i: (i, 0)),
            pl.BlockSpec((CONV_HALO, c), lambda i: (jnp.maximum(i * per - 1, 0), 0)),
            pl.BlockSpec((None, k_taps, c), lambda i: (jm, 0, 0)),
            pl.BlockSpec((None, 1, c), lambda i: (jm, 0, 0)),
            pl.BlockSpec((None, 1, c), lambda i: (jm, 0, 0)),
            pl.BlockSpec((None, 1, c), lambda i: (jm, 0, 0)),
            _mod_spec(layer, N_MOD + 2, d),
            pl.BlockSpec((None, c, d), lambda i: (jm, 0, 0), pipeline_mode=pl.Buffered(1)),
            pl.BlockSpec((None, 1, d), lambda i: (jm, 0, 0)),
        ],
        out_specs=pl.BlockSpec((tm, d), lambda i: (i, 0)),
        out_shape=jax.ShapeDtypeStruct((m, d), F32),
        scratch_shapes=[pltpu.VMEM((tm + CONV_HALO, c), F32), pltpu.VMEM((tm, c), F32), pltpu.VMEM((tm, c), BF16)],
        compiler_params=pltpu.CompilerParams(dimension_semantics=("parallel",), vmem_limit_bytes=vmem),
        name="conv_mix",
    )(x2d, y, y, dw_w, dw_b[:, None, :], ln_g[:, None, :], ln_b[:, None, :], mods, w_out, b_out[:, None, :])


def _conv_mixer_kernel(xc_ref, xr_ref, g_ref, shift_ref, scale_ref, gate_ref, wi_ref, bi_ref, dww_ref, dwb_ref,
                       lng_ref, lnb_ref, wo_ref, bo_ref, o_ref, h_ref, win_ref, s_ref, op_ref,
                       *, n_blocks, blocks_per_batch, rows):
    step = pl.program_id(0)
    _, n_slabs, _, lanes = win_ref.shape
    tm = s_ref.shape[0]
    n_tiles, _, tn = op_ref.shape
    yc_ref = win_ref.at[2]
    k_taps = dww_ref.shape[1]
    base = CONV_HALO - (k_taps - 1)
    slabs_per_tile = tn // lanes
    rd = step % 2

    @pl.when(step == 0)
    def _():
        win_ref[...] = jnp.zeros_like(win_ref)
        s_ref[...] = jnp.zeros_like(s_ref)

    b_in = jnp.minimum(step, n_blocks - 1) // blocks_per_batch
    h_ref[...] = _norm_mod_rows(xc_ref[...], g_ref[...], shift_ref[pl.ds(b_in, 1), :],
                                scale_ref[pl.ds(b_in, 1), :]).astype(BF16)

    def conv_slab(sl):
        for r0 in range(0, tm, rows):
            acc = jnp.broadcast_to(dwb_ref[sl], (rows, lanes))
            for p in range(SUBLANES):
                part = None
                for a8 in range(0, CONV_HALO + SUBLANES, SUBLANES):
                    k = a8 + p - base
                    if 0 <= k < k_taps:
                        n_load = rows + (SUBLANES if p else 0)
                        term = dww_ref[sl, k:k + 1, :] * win_ref[rd, sl, r0 + a8:r0 + a8 + n_load, :]
                        part = term if part is None else part + term
                acc = acc + part[p:p + rows]
            yc_ref[sl, r0:r0 + rows, :] = acc

    def tile(j, carry):
        for t in range(slabs_per_tile):
            conv_slab(j * slabs_per_tile + t)
        h = h_ref[...]
        op_ref[j] = _dot(s_ref[...], wo_ref[j])
        a = _dot(h, wi_ref[j]) + bi_ref[j]
        gl = _dot(h, wi_ref[n_tiles + j]) + bi_ref[n_tiles + j]
        y = a * jax.nn.sigmoid(gl)
        for t in range(slabs_per_tile):
            win_ref[1 - rd, j * slabs_per_tile + t, CONV_HALO:, :] = y[:, t * lanes:(t + 1) * lanes]
        return carry

    lax.fori_loop(0, n_tiles, tile, 0)

    blk_a = jnp.maximum(step - 2, 0)
    gate = gate_ref[pl.ds(blk_a // blocks_per_batch, 1), :]
    for j in range(n_tiles):
        ns = slice(j * tn, (j + 1) * tn)
        o_ref[:, ns] = xr_ref[:, ns] + gate[:, ns] * (op_ref[j] + bo_ref[:, ns])

    for r0 in range(0, tm, rows):
        tot = yc_ref[0, r0:r0 + rows, :]
        for sl in range(1, n_slabs):
            tot = tot + yc_ref[sl, r0:r0 + rows, :]
        mu = jnp.sum(tot, axis=-1, keepdims=True) * (1.0 / (n_slabs * lanes))
        sq = None
        for sl in range(n_slabs):
            dv = yc_ref[sl, r0:r0 + rows, :] - mu
            sq = dv * dv if sq is None else sq + dv * dv
        var = jnp.sum(sq, axis=-1, keepdims=True) * (1.0 / (n_slabs * lanes))
        rstd = lax.rsqrt(var + EPS)
        for sl in range(n_slabs):
            cs = slice(sl * lanes, (sl + 1) * lanes)
            yl = (yc_ref[sl, r0:r0 + rows, :] - mu) * rstd * lng_ref[:, cs] + lnb_ref[:, cs]
            s_ref[r0:r0 + rows, cs] = (yl * jax.nn.sigmoid(yl)).astype(BF16)

    for sl in range(n_slabs):
        tail = win_ref[rd, sl, tm:tm + CONV_HALO, :]
        win_ref[1 - rd, sl, 0:CONV_HALO, :] = jnp.where(step % blocks_per_batch == 0, 0.0, tail)


def _conv_mixer(x2d, mods, norm_g3, w_in, b_in, dw_w, dw_b, ln_g, ln_b, w_out, b_out, layer, jm, seq,
                tm=256, rows=64, tn=256):
    m, d = x2d.shape
    c = w_out.shape[0]
    k_taps = dw_w.shape[1]
    assert k_taps - 1 <= CONV_HALO and seq % tm == 0 and tm % rows == 0 and c % tn == 0 and d == c
    assert tn % V7X_LANES == 0
    n_blocks, n_tiles, n_slabs = m // tm, c // tn, c // V7X_LANES
    wi3 = w_in.reshape(d, 2 * n_tiles, tn).transpose(1, 0, 2)
    wo3 = w_out.reshape(c, n_tiles, tn).transpose(1, 0, 2)
    bi3 = b_in[jm].reshape(2 * n_tiles, 1, tn)
    dww3 = dw_w[jm].reshape(k_taps, n_slabs, V7X_LANES).transpose(1, 0, 2)
    dwb3 = dw_b[jm].reshape(n_slabs, 1, V7X_LANES)
    kern = functools.partial(_conv_mixer_kernel, n_blocks=n_blocks, blocks_per_batch=seq // tm, rows=rows)
    vmem = _vmem_limit(
        pipelined=3 * _nbytes((tm, d), F32) + 6 * _nbytes((MOD_ROWS, d), F32) + _nbytes((n_slabs, 32, V7X_LANES), F32)
        + _nbytes((2 * n_tiles, MOD_ROWS, tn), F32) + _nbytes((n_slabs, MOD_ROWS, V7X_LANES), F32),
        single=_nbytes((d, 2 * c), BF16) + _nbytes((c, d), BF16),
        scratch=_nbytes((tm, d), BF16) + _nbytes((2 * (tm + CONV_HALO), c), F32) + 2 * _nbytes((tm, c), F32)
        + _nbytes((tm, c), BF16),
        temps=4 * _nbytes((tm, d), F32))
    cur = lambda i: (jnp.minimum(i, n_blocks - 1), 0)
    lag2 = lambda i: (jnp.maximum(i - 2, 0), 0)
    whole = lambda i: (0, 0, 0)
    return pl.pallas_call(
        kern,
        grid=(n_blocks + 2,),
        in_specs=[
            pl.BlockSpec((tm, d), cur),
            pl.BlockSpec((tm, d), lag2),
            pl.BlockSpec((None, 1, d), lambda i: (layer * 3 + 1, 0, 0)),
            _mod_spec(layer, N_MOD + 0, d),
            _mod_spec(layer, N_MOD + 1, d),
            _mod_spec(layer, N_MOD + 2, d),
            pl.BlockSpec((2 * n_tiles, d, tn), whole, pipeline_mode=pl.Buffered(1)),
            pl.BlockSpec((2 * n_tiles, 1, tn), whole),
            pl.BlockSpec((n_slabs, k_taps, V7X_LANES), whole),
            pl.BlockSpec((n_slabs, 1, V7X_LANES), whole),
            pl.BlockSpec((None, 1, c), lambda i: (jm, 0, 0)),
            pl.BlockSpec((None, 1, c), lambda i: (jm, 0, 0)),
            pl.BlockSpec((n_tiles, c, tn), whole, pipeline_mode=pl.Buffered(1)),
            pl.BlockSpec((None, 1, d), lambda i: (jm, 0, 0)),
        ],
        out_specs=pl.BlockSpec((tm, d), lag2),
        out_shape=jax.ShapeDtypeStruct((m, d), F32),
        scratch_shapes=[pltpu.VMEM((tm, d), BF16),
                        pltpu.VMEM((3, n_slabs, tm + CONV_HALO, V7X_LANES), F32),
                        pltpu.VMEM((tm, c), BF16),
                        pltpu.VMEM((n_tiles, tm, tn), F32)],
        compiler_params=pltpu.CompilerParams(dimension_semantics=("arbitrary",), vmem_limit_bytes=vmem),
        name="conv_mixer",
    )(x2d, x2d, norm_g3, mods, mods, mods, wi3, bi3, dww3, dwb3, ln_g[:, None, :], ln_b[:, None, :], wo3,
      b_out[:, None, :])


def _conv_block_kernel(xc_ref, xr_ref, g_ref, shift_ref, scale_ref, gate_ref, wi_ref, bi_ref, dww_ref, dwb_ref,
                       lng_ref, lnb_ref, wo_ref, bo_ref, o_ref, win_ref, yc_ref, s_ref,
                       *, n_blocks, blocks_per_batch, rows, tn):
    step = pl.program_id(0)
    tm, c = yc_ref.shape
    d = o_ref.shape[1]
    k_taps = dww_ref.shape[0]
    base = CONV_HALO - (k_taps - 1)

    @pl.when(step == 0)
    def _():
        win_ref[0:CONV_HALO, :] = jnp.zeros((CONV_HALO, c), F32)
        s_ref[...] = jnp.zeros_like(s_ref)

    b_in = jnp.minimum(step, n_blocks - 1) // blocks_per_batch
    h = _norm_mod_rows(xc_ref[...], g_ref[...], shift_ref[pl.ds(b_in, 1), :],
                       scale_ref[pl.ds(b_in, 1), :]).astype(BF16)
    gate = gate_ref[pl.ds(jnp.maximum(step - 1, 0) // blocks_per_batch, 1), :]
    s_prev = s_ref[...]

    for j0 in range(0, c, tn):
        js = slice(j0, j0 + tn)
        a = _dot(h, wi_ref[:, js]) + bi_ref[:, js]
        gl = _dot(h, wi_ref[:, c + j0:c + j0 + tn]) + bi_ref[:, c + j0:c + j0 + tn]
        win_ref[CONV_HALO:, js] = a * jax.nn.sigmoid(gl)
        o_ref[:, js] = xr_ref[:, js] + gate[:, js] * (_dot(s_prev, wo_ref[:, js]) + bo_ref[:, js])
        for l0 in range(j0, j0 + tn, V7X_LANES):
            cs = slice(l0, l0 + V7X_LANES)
            for r0 in range(0, tm, rows):
                acc = jnp.broadcast_to(dwb_ref[:, cs], (rows, V7X_LANES))
                for p in range(SUBLANES):
                    part = None
                    for a8 in range(0, CONV_HALO + SUBLANES, SUBLANES):
                        k = a8 + p - base
                        if 0 <= k < k_taps:
                            n_load = rows + (SUBLANES if p else 0)
                            term = dww_ref[k:k + 1, cs] * win_ref[r0 + a8:r0 + a8 + n_load, cs]
                            part = term if part is None else part + term
                    acc = acc + part[p:p + rows]
                yc_ref[r0:r0 + rows, cs] = acc

    lng = lng_ref[...]
    lnb = lnb_ref[...]
    for r0 in range(0, tm, rows):
        yl = _layer_norm_rows(yc_ref[r0:r0 + rows, :], lng, lnb)
        s_ref[r0:r0 + rows, :] = (yl * jax.nn.sigmoid(yl)).astype(BF16)

    tail = win_ref[tm:tm + CONV_HALO, :]
    win_ref[0:CONV_HALO, :] = jnp.where((step + 1) % blocks_per_batch == 0, 0.0, tail)


def _conv_block(x2d, mods, norm_g3, w_in, b_in, dw_w, dw_b, ln_g, ln_b, w_out, b_out, layer, jm, seq,
                tm=256, rows=64, tn=256):
    m, d = x2d.shape
    c = w_out.shape[1]
    k_taps = dw_w.shape[1]
    assert k_taps - 1 <= CONV_HALO and seq % tm == 0 and tm % rows == 0 and c % tn == 0 and d == c
    n_blocks = m // tm
    kern = functools.partial(_conv_block_kernel, n_blocks=n_blocks, blocks_per_batch=seq // tm, rows=rows, tn=tn)
    vmem = _vmem_limit(
        pipelined=3 * _nbytes((tm, d), F32) + 8 * _nbytes((MOD_ROWS, d), F32) + _nbytes((CONV_HALO, c), F32)
        + _nbytes((MOD_ROWS, 2 * c), F32),
        single=_nbytes((d, 2 * c), BF16) + _nbytes((c, d), BF16),
        scratch=_nbytes((2 * tm + CONV_HALO, c), F32) + _nbytes((tm, c), BF16),
        temps=_nbytes((tm, d), BF16) + 3 * _nbytes((tm, d), F32))
    cur = lambda i: (jnp.minimum(i, n_blocks - 1), 0)
    lag = lambda i: (jnp.maximum(i - 1, 0), 0)
    return pl.pallas_call(
        kern,
        grid=(n_blocks + 1,),
        in_specs=[
            pl.BlockSpec((tm, d), cur),
            pl.BlockSpec((tm, d), lag),
            pl.BlockSpec((None, 1, d), lambda i: (layer * 3 + 1, 0, 0)),
            _mod_spec(layer, N_MOD + 0, d),
            _mod_spec(layer, N_MOD + 1, d),
            _mod_spec(layer, N_MOD + 2, d),
            pl.BlockSpec((None, d, 2 * c), lambda i: (jm, 0, 0), pipeline_mode=pl.Buffered(1)),
            pl.BlockSpec((None, 1, 2 * c), lambda i: (jm, 0, 0)),
            pl.BlockSpec((None, k_taps, c), lambda i: (jm, 0, 0)),
            pl.BlockSpec((None, 1, c), lambda i: (jm, 0, 0)),
            pl.BlockSpec((None, 1, c), lambda i: (jm, 0, 0)),
            pl.BlockSpec((None, 1, c), lambda i: (jm, 0, 0)),
            pl.BlockSpec((None, c, d), lambda i: (jm, 0, 0), pipeline_mode=pl.Buffered(1)),
            pl.BlockSpec((None, 1, d), lambda i: (jm, 0, 0)),
        ],
        out_specs=pl.BlockSpec((tm, d), lag),
        out_shape=jax.ShapeDtypeStruct((m, d), F32),
        scratch_shapes=[pltpu.VMEM((tm + CONV_HALO, c), F32), pltpu.VMEM((tm, c), F32), pltpu.VMEM((tm, c), BF16)],
        compiler_params=pltpu.CompilerParams(dimension_semantics=("arbitrary",), vmem_limit_bytes=vmem),
        name="conv_block",
    )(x2d, x2d, norm_g3, mods, mods, mods, w_in, b_in[:, None, :], dw_w, dw_b[:, None, :], ln_g[:, None, :],
      ln_b[:, None, :], w_out, b_out[:, None, :])


def _gelu(z):
    return 0.5 * z * (1.0 + lax.erf(z * (2.0 ** -0.5)))


def _gmlp_block_kernel(xc_ref, xr_ref, g_ref, shift_ref, scale_ref, gate_ref, wi_ref, lng_ref, lnb_ref, ws_ref,
                       bs_ref, wo_ref, o_ref, u_ref, v_ref, vn_ref, s_ref, wsm_ref,
                       *, n_blocks, blocks_per_batch, rows, tn):
    step = pl.program_id(0)
    tm, e = u_ref.shape
    n_heads, length, _ = ws_ref.shape
    dh = e // n_heads

    @pl.when(step == 0)
    def _():
        s_ref[...] = jnp.zeros_like(s_ref)
        causal = (lax.broadcasted_iota(jnp.int32, (length, length), 0)
                  >= lax.broadcasted_iota(jnp.int32, (length, length), 1))
        for hd in range(n_heads):
            wsm_ref[hd] = jnp.where(causal, ws_ref[hd], 0.0).astype(BF16)

    b_in = jnp.minimum(step, n_blocks - 1) // blocks_per_batch
    h = _norm_mod_rows(xc_ref[...], g_ref[...], shift_ref[pl.ds(b_in, 1), :],
                       scale_ref[pl.ds(b_in, 1), :]).astype(BF16)
    gate = gate_ref[pl.ds(jnp.maximum(step - 1, 0) // blocks_per_batch, 1), :]
    s_prev = s_ref[...]

    for j0 in range(0, e, tn):
        js = slice(j0, j0 + tn)
        v_ref[:, js] = _gelu(_dot(h, wi_ref[:, e + j0:e + j0 + tn]))
        o_ref[:, js] = xr_ref[:, js] + gate[:, js] * _dot(s_prev, wo_ref[:, js])

    lng = lng_ref[...]
    lnb = lnb_ref[...]
    for r0 in range(0, tm, rows):
        vn_ref[r0:r0 + rows, :] = _layer_norm_rows(v_ref[r0:r0 + rows, :], lng, lnb).astype(BF16)

    for j0 in range(0, e, tn):
        u_ref[:, j0:j0 + tn] = _gelu(_dot(h, wi_ref[:, j0:j0 + tn]))

    for hd in range(n_heads):
        cs = slice(hd * dh, (hd + 1) * dh)
        bias = bs_ref[hd]
        for r0 in range(0, tm, length):
            rs = slice(r0, r0 + length)
            mixed = _dot(wsm_ref[hd], vn_ref[rs, cs]) + bias
            s_ref[rs, cs] = (u_ref[rs, cs] * mixed).astype(BF16)


def _gmlp_block(x2d, mods, norm_g3, w_in, ln_g, ln_b, ws, bs, w_out, layer, jm, seq, tm=256, rows=64, tn=256):
    m, d = x2d.shape
    e = w_out.shape[1]
    n_heads, length = ws.shape[1], ws.shape[2]
    assert length == CHUNK and seq % tm == 0 and tm % length == 0 and tm % rows == 0 and e % tn == 0
    assert (e // n_heads) % V7X_LANES == 0 and d == e
    n_blocks = m // tm
    kern = functools.partial(_gmlp_block_kernel, n_blocks=n_blocks, blocks_per_batch=seq // tm, rows=rows, tn=tn)
    vmem = _vmem_limit(
        pipelined=3 * _nbytes((tm, d), F32) + 6 * _nbytes((MOD_ROWS, d), F32)
        + 2 * _nbytes((n_heads, length, length), F32),
        single=_nbytes((d, 2 * e), BF16) + _nbytes((e, d), BF16),
        scratch=2 * _nbytes((tm, e), F32) + 2 * _nbytes((tm, e), BF16) + _nbytes((n_heads, length, length), BF16),
        temps=_nbytes((tm, d), BF16) + 3 * _nbytes((tm, d), F32))
    cur = lambda i: (jnp.minimum(i, n_blocks - 1), 0)
    lag = lambda i: (jnp.maximum(i - 1, 0), 0)
    return pl.pallas_call(
        kern,
        grid=(n_blocks + 1,),
        in_specs=[
            pl.BlockSpec((tm, d), cur),
            pl.BlockSpec((tm, d), lag),
            pl.BlockSpec((None, 1, d), lambda i: (layer * 3 + 1, 0, 0)),
            _mod_spec(layer, N_MOD + 0, d),
            _mod_spec(layer, N_MOD + 1, d),
            _mod_spec(layer, N_MOD + 2, d),
            pl.BlockSpec((None, d, 2 * e), lambda i: (jm, 0, 0), pipeline_mode=pl.Buffered(1)),
            pl.BlockSpec((None, 1, e), lambda i: (jm, 0, 0)),
            pl.BlockSpec((None, 1, e), lambda i: (jm, 0, 0)),
            pl.BlockSpec((None, n_heads, length, length), lambda i: (jm, 0, 0, 0)),
            pl.BlockSpec((None, n_heads, length, 1), lambda i: (jm, 0, 0, 0)),
            pl.BlockSpec((None, e, d), lambda i: (jm, 0, 0), pipeline_mode=pl.Buffered(1)),
        ],
        out_specs=pl.BlockSpec((tm, d), lag),
        out_shape=jax.ShapeDtypeStruct((m, d), F32),
        scratch_shapes=[pltpu.VMEM((tm, e), F32), pltpu.VMEM((tm, e), F32), pltpu.VMEM((tm, e), BF16),
                        pltpu.VMEM((tm, e), BF16), pltpu.VMEM((n_heads, length, length), BF16)],
        compiler_params=pltpu.CompilerParams(dimension_semantics=("arbitrary",), vmem_limit_bytes=vmem),
        name="gmlp_block",
    )(x2d, x2d, norm_g3, mods, mods, mods, w_in, ln_g[:, None, :], ln_b[:, None, :], ws, bs[..., None], w_out)


def kernel(x, c, ada_w, ada_b, norm_g, ffn_w_in, ffn_w_out, gm_w_in, gm_ln_g, gm_ln_b, gm_ws, gm_bs, gm_w_out,
           cv_w_in, cv_b_in, cv_dw_w, cv_dw_b, cv_ln_g, cv_ln_b, cv_w_out, cv_b_out, final_g):
    bsz, seq, d = x.shape
    depth, n_sub = norm_g.shape[0], norm_g.shape[1]
    n_mixers = 2
    assert n_sub == 3 and ada_w.shape[2] == n_sub * N_MOD * d

    mods = _ada_table(c, ada_w, ada_b)
    norm_g3 = norm_g.reshape(depth * n_sub, 1, d)
    gm_w_in, gm_w_out = gm_w_in.astype(BF16), gm_w_out.astype(BF16)
    cv_w_in, cv_w_out = cv_w_in.astype(BF16), cv_w_out.astype(BF16)

    h = x.reshape(bsz * seq, d)
    for i in range(depth):
        h = _ffn(h, mods, norm_g3, ffn_w_in, ffn_w_out, i, 0, 0, seq)
        jm = i // n_mixers
        if i % n_mixers == 0:
            h = _gmlp_block(h, mods, norm_g3, gm_w_in, gm_ln_g, gm_ln_b, gm_ws, gm_bs, gm_w_out, i, jm, seq)
        else:
            h = _conv_block(h, mods, norm_g3, cv_w_in, cv_b_in, cv_dw_w, cv_dw_b, cv_ln_g, cv_ln_b, cv_w_out,
                            cv_b_out, i, jm, seq)
        h = _ffn(h, mods, norm_g3, ffn_w_in, ffn_w_out, i, 2, 1, seq,
                 final_g=final_g if i == depth - 1 else None)
    return h.reshape(bsz, seq, d)
```

```python
import functools

import jax
import jax.numpy as jnp
from jax import lax
from jax.experimental import pallas as pl
from jax.experimental.pallas import tpu as pltpu

EPS = 1e-6
N_MOD = 3
CHUNK = 128
MOD_ROWS = 8
CONV_HALO = 32

V7X_VMEM_BYTES = 64 * 1024 * 1024
V7X_LANES = 128
SUBLANES = 8

F32 = jnp.float32
BF16 = jnp.bfloat16


def _vmem_limit(pipelined, single, scratch, temps):
    need = 2 * pipelined + single + scratch + temps
    cap = V7X_VMEM_BYTES - 1024 * 1024
    assert need <= cap, need
    return int(min(need + need // 10, cap))


def _nbytes(shape, dtype):
    n = 1
    for s in shape:
        n *= s
    return n * jnp.dtype(dtype).itemsize


def _dot(a, b):
    return jnp.dot(a, b, preferred_element_type=F32)


def _norm_mod_rows(x, g, shift, scale):
    ms = jnp.mean(x * x, axis=-1, keepdims=True)
    return x * lax.rsqrt(ms + EPS) * (g * (1.0 + scale)) + shift


def _fill_h(h_ref, x_ref, g_ref, shift_ref, scale_ref, b, rows):
    tm = x_ref.shape[0]
    g = g_ref[...]
    shift = shift_ref[pl.ds(b, 1), :]
    scale = scale_ref[pl.ds(b, 1), :]

    def body(r, carry):
        rs = pl.ds(pl.multiple_of(r * rows, rows), rows)
        h_ref[rs, :] = _norm_mod_rows(x_ref[rs, :], g, shift, scale).astype(BF16)
        return carry

    lax.fori_loop(0, tm // rows, body, 0)


def _ada_kernel(c_ref, w_ref, b_ref, o_ref):
    c = c_ref[...]
    cond = (c * jax.nn.sigmoid(c)).astype(BF16)
    o_ref[...] = _dot(cond, w_ref[...].astype(BF16)) + b_ref[...]


def _ada_table(c, ada_w, ada_b, tn=1024):
    depth, d, n = ada_w.shape
    bsz = c.shape[0]
    assert bsz <= MOD_ROWS and n % d == 0 and d % tn == 0
    per = d // tn
    c8 = jnp.pad(c, ((0, MOD_ROWS - bsz), (0, 0)))
    b4 = ada_b.reshape(depth, n // d, 1, d)
    return pl.pallas_call(
        _ada_kernel,
        grid=(depth, n // tn),
        in_specs=[
            pl.BlockSpec((MOD_ROWS, d), lambda l, j: (0, 0)),
            pl.BlockSpec((None, d, tn), lambda l, j: (l, 0, j)),
            pl.BlockSpec((None, None, 1, tn), lambda l, j: (l, j // per, 0, j % per)),
        ],
        out_specs=pl.BlockSpec((None, None, MOD_ROWS, tn), lambda l, j: (l, j // per, 0, j % per)),
        out_shape=jax.ShapeDtypeStruct((depth, n // d, MOD_ROWS, d), F32),
        compiler_params=pltpu.CompilerParams(
            dimension_semantics=("parallel", "parallel"),
            vmem_limit_bytes=_vmem_limit(
                _nbytes((d, tn), F32) + _nbytes((MOD_ROWS, d + 2 * tn), F32), 0, 0,
                2 * _nbytes((d, tn), BF16) + 4 * _nbytes((MOD_ROWS, d), F32))),
        name="ada",
    )(c8, ada_w, b4)


def _mod_spec(layer, row, d):
    return pl.BlockSpec((None, None, MOD_ROWS, d), lambda *_: (layer, row, 0, 0))


def _ffn_kernel(x_hbm, g_ref, shift_ref, scale_ref, gate_ref, wg_ref, wu_ref, wo_ref, *rest,
                blocks_per_batch, n_i, n_f, n_chunk, rows, final):
    if final:
        fg_ref, o_ref, h_ref, xbuf_ref, sem = rest
    else:
        o_ref, h_ref, xbuf_ref, sem = rest
    i = pl.program_id(0)
    f = pl.program_id(1)
    b = i // blocks_per_batch
    tm, d = o_ref.shape

    def x_copy(blk):
        return pltpu.make_async_copy(x_hbm.at[pl.ds(blk * tm, tm), :], xbuf_ref, sem.at[0])

    @pl.when(f == 0)
    def _():
        @pl.when(i == 0)
        def _():
            x_copy(i).start()

        x_copy(i).wait()
        g = g_ref[...]
        shift = shift_ref[pl.ds(b, 1), :]
        scale = scale_ref[pl.ds(b, 1), :]
        for r0 in range(0, tm, rows):
            rs = slice(r0, r0 + rows)
            x = xbuf_ref[rs, :]
            h_ref[rs, :] = _norm_mod_rows(x, g, shift, scale).astype(BF16)
            o_ref[rs, :] = x

    @pl.when((f == 1) & (i + 1 < n_i))
    def _():
        x_copy(i + 1).start()

    half_gate = 0.5 * gate_ref[pl.ds(b, 1), :]
    h = h_ref[...]
    gt = _dot(h, wg_ref[...].astype(BF16))
    up = _dot(h, wu_ref[...].astype(BF16))
    a = (gt * jax.nn.sigmoid(gt) * up).astype(BF16)
    for n0 in range(0, d, n_chunk):
        ns = slice(n0, n0 + n_chunk)
        o_ref[:, ns] += half_gate[:, ns] * _dot(a, wo_ref[:, ns].astype(BF16))

    if final:
        @pl.when(f == n_f - 1)
        def _():
            fg = fg_ref[...]

            def body(r, carry):
                rs = pl.ds(pl.multiple_of(r * rows, rows), rows)
                o = o_ref[rs, :]
                ms = jnp.mean(o * o, axis=-1, keepdims=True)
                o_ref[rs, :] = o * lax.rsqrt(ms + EPS) * fg
                return carry

            lax.fori_loop(0, tm // rows, body, 0)


def _ffn(x2d, mods, norm_g3, w_in, w_out, layer, sub, which, seq, final_g=None, tm=1024, tf=512,
         n_chunk=512, rows=128):
    m, d = x2d.shape
    f_dim = w_out.shape[2]
    assert seq % tm == 0 and f_dim % tf == 0 and d % n_chunk == 0 and tm % (2 * rows) == 0
    n_f = f_dim // tf
    row0 = sub * N_MOD
    in_specs = [
        pl.BlockSpec(memory_space=pl.ANY),
        pl.BlockSpec((None, 1, d), lambda i, f: (layer * 3 + sub, 0, 0)),
        _mod_spec(layer, row0 + 0, d),
        _mod_spec(layer, row0 + 1, d),
        _mod_spec(layer, row0 + 2, d),
        pl.BlockSpec((None, None, d, tf), lambda i, f: (layer, which, 0, f)),
        pl.BlockSpec((None, None, d, tf), lambda i, f: (layer, which, 0, f + n_f)),
        pl.BlockSpec((None, None, tf, d), lambda i, f: (layer, which, f, 0)),
    ]
    args = [x2d, norm_g3, mods, mods, mods, w_in, w_in, w_out]
    if final_g is not None:
        in_specs.append(pl.BlockSpec((1, d), lambda i, f: (0, 0)))
        args.append(final_g.reshape(1, d))
    kern = functools.partial(_ffn_kernel, blocks_per_batch=seq // tm, n_i=m // tm, n_f=n_f, n_chunk=n_chunk,
                             rows=rows, final=final_g is not None)
    vmem = _vmem_limit(
        pipelined=_nbytes((tm, d), F32) + 3 * _nbytes((d, tf), w_in.dtype) + 5 * _nbytes((MOD_ROWS, d), F32),
        single=0,
        scratch=_nbytes((tm, d), BF16) + _nbytes((tm, d), F32),
        temps=2 * _nbytes((tm, tf), F32) + _nbytes((tm, tf), BF16))
    return pl.pallas_call(
        kern,
        grid=(m // tm, n_f),
        in_specs=in_specs,
        out_specs=pl.BlockSpec((tm, d), lambda i, f: (i, 0)),
        out_shape=jax.ShapeDtypeStruct((m, d), F32),
        scratch_shapes=[pltpu.VMEM((tm, d), BF16), pltpu.VMEM((tm, d), F32), pltpu.SemaphoreType.DMA((1,))],
        compiler_params=pltpu.CompilerParams(
            dimension_semantics=("arbitrary", "arbitrary"), vmem_limit_bytes=vmem),
        name="ffn",
    )(*args)


def _gmlp_proj_kernel(x_ref, g_ref, shift_ref, scale_ref, w_ref, z_ref, h_ref, *, blocks_per_batch, rows):
    i = pl.program_id(0)

    @pl.when(pl.program_id(1) == 0)
    def _():
        _fill_h(h_ref, x_ref, g_ref, shift_ref, scale_ref, i // blocks_per_batch, rows)

    z = _dot(h_ref[...], w_ref[...])
    z_ref[...] = 0.5 * z * (1.0 + lax.erf(z * (2.0 ** -0.5)))


def _gmlp_proj(x2d, mods, norm_g3, w_in, layer, jm, seq, tm=1024, tn=512, rows=128):
    m, d = x2d.shape
    n = w_in.shape[2]
    assert seq % tm == 0 and n % tn == 0
    kern = functools.partial(_gmlp_proj_kernel, blocks_per_batch=seq // tm, rows=rows)
    vmem = _vmem_limit(
        pipelined=_nbytes((tm, d), F32) + _nbytes((d, tn), BF16) + _nbytes((tm, tn), F32)
        + 3 * _nbytes((MOD_ROWS, d), F32),
        single=0, scratch=_nbytes((tm, d), BF16), temps=5 * _nbytes((tm, tn), F32))
    return pl.pallas_call(
        kern,
        grid=(m // tm, n // tn),
        in_specs=[
            pl.BlockSpec((tm, d), lambda i, j: (i, 0)),
            pl.BlockSpec((None, 1, d), lambda i, j: (layer * 3 + 1, 0, 0)),
            _mod_spec(layer, N_MOD + 0, d),
            _mod_spec(layer, N_MOD + 1, d),
            pl.BlockSpec((None, d, tn), lambda i, j: (jm, 0, j)),
        ],
        out_specs=pl.BlockSpec((tm, tn), lambda i, j: (i, j)),
        out_shape=jax.ShapeDtypeStruct((m, n), F32),
        scratch_shapes=[pltpu.VMEM((tm, d), BF16)],
        compiler_params=pltpu.CompilerParams(
            dimension_semantics=("parallel", "arbitrary"), vmem_limit_bytes=vmem),
        name="gmlp_proj",
    )(x2d, norm_g3, mods, mods, w_in)


def _layer_norm_rows(v, g, b):
    mu = jnp.mean(v, axis=-1, keepdims=True)
    dv = v - mu
    var = jnp.mean(dv * dv, axis=-1, keepdims=True)
    return dv * lax.rsqrt(var + EPS) * g + b


def _gmlp_mix_kernel(x_ref, u_ref, v_ref, lng_ref, lnb_ref, ws_ref, bs_ref, gate_ref, wo_ref, o_ref, s_ref,
                     *, blocks_per_batch, n_chunk):
    b = pl.program_id(0) // blocks_per_batch
    tm, e = s_ref.shape
    n_heads, length, _ = ws_ref.shape
    dh = e // n_heads
    lng = lng_ref[...]
    lnb = lnb_ref[...]
    causal = (lax.broadcasted_iota(jnp.int32, (length, length), 0)
              >= lax.broadcasted_iota(jnp.int32, (length, length), 1))

    def chunk(c, carry):
        rs = pl.ds(pl.multiple_of(c * length, length), length)
        vn = _layer_norm_rows(v_ref[rs, :], lng, lnb).astype(BF16)
        for hd in range(n_heads):
            cs = slice(hd * dh, (hd + 1) * dh)
            w_h = jnp.where(causal, ws_ref[hd], 0.0).astype(BF16)
            mixed = _dot(w_h, vn[:, cs]) + bs_ref[hd]
            s_ref[rs, cs] = (u_ref[rs, cs] * mixed).astype(BF16)
        return carry

    lax.fori_loop(0, tm // length, chunk, 0)
    gate = gate_ref[pl.ds(b, 1), :]
    s = s_ref[...]
    for n0 in range(0, o_ref.shape[1], n_chunk):
        ns = slice(n0, n0 + n_chunk)
        o_ref[:, ns] = x_ref[:, ns] + gate[:, ns] * _dot(s, wo_ref[:, ns])


def _gmlp_mix(x2d, z, mods, ln_g, ln_b, ws, bs, w_out, layer, jm, seq, tm=512, n_chunk=512):
    m, d = x2d.shape
    e = w_out.shape[1]
    n_heads, length = ws.shape[1], ws.shape[2]
    assert length == CHUNK and seq % tm == 0 and tm % length == 0 and (e // n_heads) % V7X_LANES == 0
    assert d % n_chunk == 0
    kern = functools.partial(_gmlp_mix_kernel, blocks_per_batch=seq // tm, n_chunk=n_chunk)
    vmem = _vmem_limit(
        pipelined=_nbytes((tm, d), F32) * 2 + 2 * _nbytes((tm, e), F32) + 2 * _nbytes((n_heads, length, length), F32)
        + 3 * _nbytes((MOD_ROWS, d), F32),
        single=_nbytes((e, d), BF16), scratch=_nbytes((tm, e), BF16),
        temps=3 * _nbytes((tm, n_chunk), F32) + 6 * _nbytes((length, e), F32))
    return pl.pallas_call(
        kern,
        grid=(m // tm,),
        in_specs=[
            pl.BlockSpec((tm, d), lambda i: (i, 0)),
            pl.BlockSpec((tm, e), lambda i: (i, 0)),
            pl.BlockSpec((tm, e), lambda i: (i, 1)),
            pl.BlockSpec((None, 1, e), lambda i: (jm, 0, 0)),
            pl.BlockSpec((None, 1, e), lambda i: (jm, 0, 0)),
            pl.BlockSpec((None, n_heads, length, length), lambda i: (jm, 0, 0, 0)),
            pl.BlockSpec((None, n_heads, length, 1), lambda i: (jm, 0, 0, 0)),
            _mod_spec(layer, N_MOD + 2, d),
            pl.BlockSpec((None, e, d), lambda i: (jm, 0, 0), pipeline_mode=pl.Buffered(1)),
        ],
        out_specs=pl.BlockSpec((tm, d), lambda i: (i, 0)),
        out_shape=jax.ShapeDtypeStruct((m, d), F32),
        scratch_shapes=[pltpu.VMEM((tm, e), BF16)],
        compiler_params=pltpu.CompilerParams(dimension_semantics=("parallel",), vmem_limit_bytes=vmem),
        name="gmlp_mix",
    )(x2d, z, z, ln_g[:, None, :], ln_b[:, None, :], ws, bs[..., None], mods, w_out)


def _conv_proj_kernel(x_ref, g_ref, shift_ref, scale_ref, wa_ref, wg_ref, ba_ref, bg_ref, y_ref, h_ref,
                      *, blocks_per_batch, rows):
    i = pl.program_id(0)

    @pl.when(pl.program_id(1) == 0)
    def _():
        _fill_h(h_ref, x_ref, g_ref, shift_ref, scale_ref, i // blocks_per_batch, rows)

    h = h_ref[...]
    a = _dot(h, wa_ref[...]) + ba_ref[...]
    gl = _dot(h, wg_ref[...]) + bg_ref[...]
    y_ref[...] = a * jax.nn.sigmoid(gl)


def _conv_proj(x2d, mods, norm_g3, w_in, b_in, layer, jm, seq, tm=1024, tn=512, rows=128):
    m, d = x2d.shape
    c = w_in.shape[2] // 2
    assert seq % tm == 0 and c % tn == 0
    n_j = c // tn
    kern = functools.partial(_conv_proj_kernel, blocks_per_batch=seq // tm, rows=rows)
    vmem = _vmem_limit(
        pipelined=_nbytes((tm, d), F32) + 2 * _nbytes((d, tn), BF16) + _nbytes((tm, tn), F32)
        + 3 * _nbytes((MOD_ROWS, d), F32) + 2 * _nbytes((MOD_ROWS, tn), F32),
        single=0, scratch=_nbytes((tm, d), BF16), temps=6 * _nbytes((tm, tn), F32))
    return pl.pallas_call(
        kern,
        grid=(m // tm, n_j),
        in_specs=[
            pl.BlockSpec((tm, d), lambda i, j: (i, 0)),
            pl.BlockSpec((None, 1, d), lambda i, j: (layer * 3 + 1, 0, 0)),
            _mod_spec(layer, N_MOD + 0, d),
            _mod_spec(layer, N_MOD + 1, d),
            pl.BlockSpec((None, d, tn), lambda i, j: (jm, 0, j)),
            pl.BlockSpec((None, d, tn), lambda i, j: (jm, 0, j + n_j)),
            pl.BlockSpec((None, 1, tn), lambda i, j: (jm, 0, j)),
            pl.BlockSpec((None, 1, tn), lambda i, j: (jm, 0, j + n_j)),
        ],
        out_specs=pl.BlockSpec((tm, tn), lambda i, j: (i, j)),
        out_shape=jax.ShapeDtypeStruct((m, c), F32),
        scratch_shapes=[pltpu.VMEM((tm, d), BF16)],
        compiler_params=pltpu.CompilerParams(
            dimension_semantics=("parallel", "arbitrary"), vmem_limit_bytes=vmem),
        name="conv_proj",
    )(x2d, norm_g3, mods, mods, w_in, w_in, b_in[:, None, :], b_in[:, None, :])


def _conv_mix_kernel(x_ref, y_ref, yprev_ref, dww_ref, dwb_ref, lng_ref, lnb_ref, gate_ref, wo_ref, bo_ref,
                     o_ref, win_ref, yc_ref, s_ref, *, blocks_per_batch, rows, n_chunk):
    i = pl.program_id(0)
    b = i // blocks_per_batch
    tm, c = y_ref.shape
    k_taps = dww_ref.shape[0]
    first = (i % blocks_per_batch) == 0

    win_ref[0:CONV_HALO, :] = jnp.where(first, 0.0, yprev_ref[...])
    win_ref[CONV_HALO:, :] = y_ref[...]

    lng = lng_ref[...]
    lnb = lnb_ref[...]
    base = CONV_HALO - (k_taps - 1)

    def chunk(r, carry):
        r0 = pl.multiple_of(r * rows, rows)
        for s in range(c // V7X_LANES):
            cs = slice(s * V7X_LANES, (s + 1) * V7X_LANES)
            acc = jnp.broadcast_to(dwb_ref[:, cs], (rows, V7X_LANES))
            for p in range(SUBLANES):
                part = None
                for a in range(CONV_HALO // SUBLANES + 1):
                    k = a * SUBLANES + p - base
                    if 0 <= k < k_taps:
                        n_load = rows + (SUBLANES if p else 0)
                        term = dww_ref[k:k + 1, cs] * win_ref[pl.ds(r0 + a * SUBLANES, n_load), cs]
                        part = term if part is None else part + term
                acc = acc + part[p:p + rows]
            yc_ref[pl.ds(r0, rows), cs] = acc
        yl = _layer_norm_rows(yc_ref[pl.ds(r0, rows), :], lng, lnb)
        s_ref[pl.ds(r0, rows), :] = (yl * jax.nn.sigmoid(yl)).astype(BF16)
        return carry

    lax.fori_loop(0, tm // rows, chunk, 0)
    gate = gate_ref[pl.ds(b, 1), :]
    s = s_ref[...]
    for n0 in range(0, o_ref.shape[1], n_chunk):
        ns = slice(n0, n0 + n_chunk)
        o_ref[:, ns] = x_ref[:, ns] + gate[:, ns] * (_dot(s, wo_ref[:, ns]) + bo_ref[:, ns])


def _conv_mix(x2d, y, mods, dw_w, dw_b, ln_g, ln_b, w_out, b_out, layer, jm, seq, tm=512, rows=64,
              n_chunk=512):
    m, d = x2d.shape
    c = y.shape[1]
    k_taps = dw_w.shape[1]
    assert k_taps - 1 <= CONV_HALO and seq % tm == 0 and tm % CONV_HALO == 0 and tm % rows == 0
    assert d % n_chunk == 0
    per = tm // CONV_HALO
    kern = functools.partial(_conv_mix_kernel, blocks_per_batch=seq // tm, rows=rows, n_chunk=n_chunk)
    vmem = _vmem_limit(
        pipelined=2 * _nbytes((tm, d), F32) + _nbytes((tm, c), F32) + _nbytes((CONV_HALO, c), F32)
        + _nbytes((CONV_HALO, c), F32) + 6 * _nbytes((MOD_ROWS, d), F32),
        single=_nbytes((c, d), BF16),
        scratch=_nbytes((tm + CONV_HALO, c), F32) + _nbytes((tm, c), F32) + _nbytes((tm, c), BF16),
        temps=4 * _nbytes((tm, n_chunk), F32) + 8 * _nbytes((rows, c), F32))
    return pl.pallas_call(
        kern,
        grid=(m // tm,),
        in_specs=[
            pl.BlockSpec((tm, d), lambda i: (i, 0)),
            pl.BlockSpec((tm, c), lambda i: (i, 0)),
            pl.BlockSpec((CONV_HALO, c), lambda i: (jnp.maximum(i * per - 1, 0), 0)),
            pl.BlockSpec((None, k_taps, c), lambda i: (jm, 0, 0)),
            pl.BlockSpec((None, 1, c), lambda i: (jm, 0, 0)),
            pl.BlockSpec((None, 1, c), lambda i: (jm, 0, 0)),
            pl.BlockSpec((None, 1, c), lambda i: (jm, 0, 0)),
            _mod_spec(layer, N_MOD + 2, d),
            pl.BlockSpec((None, c, d), lambda i: (jm, 0, 0), pipeline_mode=pl.Buffered(1)),
            pl.BlockSpec((None, 1, d), lambda i: (jm, 0, 0)),
        ],
        out_specs=pl.BlockSpec((tm, d), lambda i: (i, 0)),
        out_shape=jax.ShapeDtypeStruct((m, d), F32),
        scratch_shapes=[pltpu.VMEM((tm + CONV_HALO, c), F32), pltpu.VMEM((tm, c), F32), pltpu.VMEM((tm, c), BF16)],
        compiler_params=pltpu.CompilerParams(dimension_semantics=("parallel",), vmem_limit_bytes=vmem),
        name="conv_mix",
    )(x2d, y, y, dw_w, dw_b[:, None, :], ln_g[:, None, :], ln_b[:, None, :], mods, w_out, b_out[:, None, :])


def _conv_mixer_kernel(xc_ref, xr_ref, g_ref, shift_ref, scale_ref, gate_ref, wi_ref, bi_ref, dww_ref, dwb_ref,
                       lng_ref, lnb_ref, wo_ref, bo_ref, o_ref, h_ref, win_ref, s_ref, op_ref,
                       *, n_blocks, blocks_per_batch, rows):
    step = pl.program_id(0)
    _, n_slabs, _, lanes = win_ref.shape
    tm = s_ref.shape[0]
    n_tiles, _, tn = op_ref.shape
    yc_ref = win_ref.at[2]
    k_taps = dww_ref.shape[1]
    base = CONV_HALO - (k_taps - 1)
    slabs_per_tile = tn // lanes
    rd = step % 2

    @pl.when(step == 0)
    def _():
        win_ref[...] = jnp.zeros_like(win_ref)
        s_ref[...] = jnp.zeros_like(s_ref)

    b_in = jnp.minimum(step, n_blocks - 1) // blocks_per_batch
    h_ref[...] = _norm_mod_rows(xc_ref[...], g_ref[...], shift_ref[pl.ds(b_in, 1), :],
                                scale_ref[pl.ds(b_in, 1), :]).astype(BF16)

    def conv_slab(sl):
        for r0 in range(0, tm, rows):
            acc = jnp.broadcast_to(dwb_ref[sl], (rows, lanes))
            for p in range(SUBLANES):
                part = None
                for a8 in range(0, CONV_HALO + SUBLANES, SUBLANES):
                    k = a8 + p - base
                    if 0 <= k < k_taps:
                        n_load = rows + (SUBLANES if p else 0)
                        term = dww_ref[sl, k:k + 1, :] * win_ref[rd, sl, r0 + a8:r0 + a8 + n_load, :]
                        part = term if part is None else part + term
                acc = acc + part[p:p + rows]
            yc_ref[sl, r0:r0 + rows, :] = acc

    def tile(j, carry):
        for t in range(slabs_per_tile):
            conv_slab(j * slabs_per_tile + t)
        h = h_ref[...]
        op_ref[j] = _dot(s_ref[...], wo_ref[j])
        a = _dot(h, wi_ref[j]) + bi_ref[j]
        gl = _dot(h, wi_ref[n_tiles + j]) + bi_ref[n_tiles + j]
        y = a * jax.nn.sigmoid(gl)
        for t in range(slabs_per_tile):
            win_ref[1 - rd, j * slabs_per_tile + t, CONV_HALO:, :] = y[:, t * lanes:(t + 1) * lanes]
        return carry

    lax.fori_loop(0, n_tiles, tile, 0)

    blk_a = jnp.maximum(step - 2, 0)
    gate = gate_ref[pl.ds(blk_a // blocks_per_batch, 1), :]
    for j in range(n_tiles):
        ns = slice(j * tn, (j + 1) * tn)
        o_ref[:, ns] = xr_ref[:, ns] + gate[:, ns] * (op_ref[j] + bo_ref[:, ns])

    for r0 in range(0, tm, rows):
        tot = yc_ref[0, r0:r0 + rows, :]
        for sl in range(1, n_slabs):
            tot = tot + yc_ref[sl, r0:r0 + rows, :]
        mu = jnp.sum(tot, axis=-1, keepdims=True) * (1.0 / (n_slabs * lanes))
        sq = None
        for sl in range(n_slabs):
            dv = yc_ref[sl, r0:r0 + rows, :] - mu
            sq = dv * dv if sq is None else sq + dv * dv
        var = jnp.sum(sq, axis=-1, keepdims=True) * (1.0 / (n_slabs * lanes))
        rstd = lax.rsqrt(var + EPS)
        for sl in range(n_slabs):
            cs = slice(sl * lanes, (sl + 1) * lanes)
            yl = (yc_ref[sl, r0:r0 + rows, :] - mu) * rstd * lng_ref[:, cs] + lnb_ref[:, cs]
            s_ref[r0:r0 + rows, cs] = (yl * jax.nn.sigmoid(yl)).astype(BF16)

    for sl in range(n_slabs):
        tail = win_ref[rd, sl, tm:tm + CONV_HALO, :]
        win_ref[1 - rd, sl, 0:CONV_HALO, :] = jnp.where(step % blocks_per_batch == 0, 0.0, tail)


def _conv_mixer(x2d, mods, norm_g3, w_in, b_in, dw_w, dw_b, ln_g, ln_b, w_out, b_out, layer, jm, seq,
                tm=256, rows=64, tn=256):
    m, d = x2d.shape
    c = w_out.shape[0]
    k_taps = dw_w.shape[1]
    assert k_taps - 1 <= CONV_HALO and seq % tm == 0 and tm % rows == 0 and c % tn == 0 and d == c
    assert tn % V7X_LANES == 0
    n_blocks, n_tiles, n_slabs = m // tm, c // tn, c // V7X_LANES
    wi3 = w_in.reshape(d, 2 * n_tiles, tn).transpose(1, 0, 2)
    wo3 = w_out.reshape(c, n_tiles, tn).transpose(1, 0, 2)
    bi3 = b_in[jm].reshape(2 * n_tiles, 1, tn)
    dww3 = dw_w[jm].reshape(k_taps, n_slabs, V7X_LANES).transpose(1, 0, 2)
    dwb3 = dw_b[jm].reshape(n_slabs, 1, V7X_LANES)
    kern = functools.partial(_conv_mixer_kernel, n_blocks=n_blocks, blocks_per_batch=seq // tm, rows=rows)
    vmem = _vmem_limit(
        pipelined=3 * _nbytes((tm, d), F32) + 6 * _nbytes((MOD_ROWS, d), F32) + _nbytes((n_slabs, 32, V7X_LANES), F32)
        + _nbytes((2 * n_tiles, MOD_ROWS, tn), F32) + _nbytes((n_slabs, MOD_ROWS, V7X_LANES), F32),
        single=_nbytes((d, 2 * c), BF16) + _nbytes((c, d), BF16),
        scratch=_nbytes((tm, d), BF16) + _nbytes((2 * (tm + CONV_HALO), c), F32) + 2 * _nbytes((tm, c), F32)
        + _nbytes((tm, c), BF16),
        temps=4 * _nbytes((tm, d), F32))
    cur = lambda i: (jnp.minimum(i, n_blocks - 1), 0)
    lag2 = lambda i: (jnp.maximum(i - 2, 0), 0)
    whole = lambda i: (0, 0, 0)
    return pl.pallas_call(
        kern,
        grid=(n_blocks + 2,),
        in_specs=[
            pl.BlockSpec((tm, d), cur),
            pl.BlockSpec((tm, d), lag2),
            pl.BlockSpec((None, 1, d), lambda i: (layer * 3 + 1, 0, 0)),
            _mod_spec(layer, N_MOD + 0, d),
            _mod_spec(layer, N_MOD + 1, d),
            _mod_spec(layer, N_MOD + 2, d),
            pl.BlockSpec((2 * n_tiles, d, tn), whole, pipeline_mode=pl.Buffered(1)),
            pl.BlockSpec((2 * n_tiles, 1, tn), whole),
            pl.BlockSpec((n_slabs, k_taps, V7X_LANES), whole),
            pl.BlockSpec((n_slabs, 1, V7X_LANES), whole),
            pl.BlockSpec((None, 1, c), lambda i: (jm, 0, 0)),
            pl.BlockSpec((None, 1, c), lambda i: (jm, 0, 0)),
            pl.BlockSpec((n_tiles, c, tn), whole, pipeline_mode=pl.Buffered(1)),
            pl.BlockSpec((None, 1, d), lambda i: (jm, 0, 0)),
        ],
        out_specs=pl.BlockSpec((tm, d), lag2),
        out_shape=jax.ShapeDtypeStruct((m, d), F32),
        scratch_shapes=[pltpu.VMEM((tm, d), BF16),
                        pltpu.VMEM((3, n_slabs, tm + CONV_HALO, V7X_LANES), F32),
                        pltpu.VMEM((tm, c), BF16),
                        pltpu.VMEM((n_tiles, tm, tn), F32)],
        compiler_params=pltpu.CompilerParams(dimension_semantics=("arbitrary",), vmem_limit_bytes=vmem),
        name="conv_mixer",
    )(x2d, x2d, norm_g3, mods, mods, mods, wi3, bi3, dww3, dwb3, ln_g[:, None, :], ln_b[:, None, :], wo3,
      b_out[:, None, :])


def _conv_block_kernel(xc_ref, xr_ref, g_ref, shift_ref, scale_ref, gate_ref, wi_ref, bi_ref, dww_ref, dwb_ref,
                       lng_ref, lnb_ref, wo_ref, bo_ref, o_ref, win_ref, yc_ref, s_ref,
                       *, n_blocks, blocks_per_batch, rows, tn):
    step = pl.program_id(0)
    tm, c = yc_ref.shape
    d = o_ref.shape[1]
    k_taps = dww_ref.shape[0]
    base = CONV_HALO - (k_taps - 1)

    @pl.when(step == 0)
    def _():
        win_ref[0:CONV_HALO, :] = jnp.zeros((CONV_HALO, c), F32)
        s_ref[...] = jnp.zeros_like(s_ref)

    b_in = jnp.minimum(step, n_blocks - 1) // blocks_per_batch
    h = _norm_mod_rows(xc_ref[...], g_ref[...], shift_ref[pl.ds(b_in, 1), :],
                       scale_ref[pl.ds(b_in, 1), :]).astype(BF16)
    gate = gate_ref[pl.ds(jnp.maximum(step - 1, 0) // blocks_per_batch, 1), :]
    s_prev = s_ref[...]

    for j0 in range(0, c, tn):
        js = slice(j0, j0 + tn)
        a = _dot(h, wi_ref[:, js]) + bi_ref[:, js]
        gl = _dot(h, wi_ref[:, c + j0:c + j0 + tn]) + bi_ref[:, c + j0:c + j0 + tn]
        win_ref[CONV_HALO:, js] = a * jax.nn.sigmoid(gl)
        o_ref[:, js] = xr_ref[:, js] + gate[:, js] * (_dot(s_prev, wo_ref[:, js]) + bo_ref[:, js])
        for l0 in range(j0, j0 + tn, V7X_LANES):
            cs = slice(l0, l0 + V7X_LANES)
            for r0 in range(0, tm, rows):
                acc = jnp.broadcast_to(dwb_ref[:, cs], (rows, V7X_LANES))
                for p in range(SUBLANES):
                    part = None
                    for a8 in range(0, CONV_HALO + SUBLANES, SUBLANES):
                        k = a8 + p - base
                        if 0 <= k < k_taps:
                            n_load = rows + (SUBLANES if p else 0)
                            term = dww_ref[k:k + 1, cs] * win_ref[r0 + a8:r0 + a8 + n_load, cs]
                            part = term if part is None else part + term
                    acc = acc + part[p:p + rows]
                yc_ref[r0:r0 + rows, cs] = acc

    lng = lng_ref[...]
    lnb = lnb_ref[...]
    for r0 in range(0, tm, rows):
        yl = _layer_norm_rows(yc_ref[r0:r0 + rows, :], lng, lnb)
        s_ref[r0:r0 + rows, :] = (yl * jax.nn.sigmoid(yl)).astype(BF16)

    tail = win_ref[tm:tm + CONV_HALO, :]
    win_ref[0:CONV_HALO, :] = jnp.where((step + 1) % blocks_per_batch == 0, 0.0, tail)


def _conv_block(x2d, mods, norm_g3, w_in, b_in, dw_w, dw_b, ln_g, ln_b, w_out, b_out, layer, jm, seq,
                tm=256, rows=64, tn=256):
    m, d = x2d.shape
    c = w_out.shape[1]
    k_taps = dw_w.shape[1]
    assert k_taps - 1 <= CONV_HALO and seq % tm == 0 and tm % rows == 0 and c % tn == 0 and d == c
    n_blocks = m // tm
    kern = functools.partial(_conv_block_kernel, n_blocks=n_blocks, blocks_per_batch=seq // tm, rows=rows, tn=tn)
    vmem = _vmem_limit(
        pipelined=3 * _nbytes((tm, d), F32) + 8 * _nbytes((MOD_ROWS, d), F32) + _nbytes((CONV_HALO, c), F32)
        + _nbytes((MOD_ROWS, 2 * c), F32),
        single=_nbytes((d, 2 * c), BF16) + _nbytes((c, d), BF16),
        scratch=_nbytes((2 * tm + CONV_HALO, c), F32) + _nbytes((tm, c), BF16),
        temps=_nbytes((tm, d), BF16) + 3 * _nbytes((tm, d), F32))
    cur = lambda i: (jnp.minimum(i, n_blocks - 1), 0)
    lag = lambda i: (jnp.maximum(i - 1, 0), 0)
    return pl.pallas_call(
        kern,
        grid=(n_blocks + 1,),
        in_specs=[
            pl.BlockSpec((tm, d), cur),
            pl.BlockSpec((tm, d), lag),
            pl.BlockSpec((None, 1, d), lambda i: (layer * 3 + 1, 0, 0)),
            _mod_spec(layer, N_MOD + 0, d),
            _mod_spec(layer, N_MOD + 1, d),
            _mod_spec(layer, N_MOD + 2, d),
            pl.BlockSpec((None, d, 2 * c), lambda i: (jm, 0, 0), pipeline_mode=pl.Buffered(1)),
            pl.BlockSpec((None, 1, 2 * c), lambda i: (jm, 0, 0)),
            pl.BlockSpec((None, k_taps, c), lambda i: (jm, 0, 0)),
            pl.BlockSpec((None, 1, c), lambda i: (jm, 0, 0)),
            pl.BlockSpec((None, 1, c), lambda i: (jm, 0, 0)),
            pl.BlockSpec((None, 1, c), lambda i: (jm, 0, 0)),
            pl.BlockSpec((None, c, d), lambda i: (jm, 0, 0), pipeline_mode=pl.Buffered(1)),
            pl.BlockSpec((None, 1, d), lambda i: (jm, 0, 0)),
        ],
        out_specs=pl.BlockSpec((tm, d), lag),
        out_shape=jax.ShapeDtypeStruct((m, d), F32),
        scratch_shapes=[pltpu.VMEM((tm + CONV_HALO, c), F32), pltpu.VMEM((tm, c), F32), pltpu.VMEM((tm, c), BF16)],
        compiler_params=pltpu.CompilerParams(dimension_semantics=("arbitrary",), vmem_limit_bytes=vmem),
        name="conv_block",
    )(x2d, x2d, norm_g3, mods, mods, mods, w_in, b_in[:, None, :], dw_w, dw_b[:, None, :], ln_g[:, None, :],
      ln_b[:, None, :], w_out, b_out[:, None, :])


def _gelu(z):
    return 0.5 * z * (1.0 + lax.erf(z * (2.0 ** -0.5)))


def _gmlp_block_kernel(xc_ref, xr_ref, g_ref, shift_ref, scale_ref, gate_ref, wi_ref, lng_ref, lnb_ref, ws_ref,
                       bs_ref, wo_ref, o_ref, u_ref, v_ref, vn_ref, s_ref, wsm_ref,
                       *, n_blocks, blocks_per_batch, rows, tn):
    step = pl.program_id(0)
    tm, e = u_ref.shape
    n_heads, length, _ = ws_ref.shape
    dh = e // n_heads

    @pl.when(step == 0)
    def _():
        s_ref[...] = jnp.zeros_like(s_ref)
        causal = (lax.broadcasted_iota(jnp.int32, (length, length), 0)
                  >= lax.broadcasted_iota(jnp.int32, (length, length), 1))
        for hd in range(n_heads):
            wsm_ref[hd] = jnp.where(causal, ws_ref[hd], 0.0).astype(BF16)

    b_in = jnp.minimum(step, n_blocks - 1) // blocks_per_batch
    h = _norm_mod_rows(xc_ref[...], g_ref[...], shift_ref[pl.ds(b_in, 1), :],
                       scale_ref[pl.ds(b_in, 1), :]).astype(BF16)
    gate = gate_ref[pl.ds(jnp.maximum(step - 1, 0) // blocks_per_batch, 1), :]
    s_prev = s_ref[...]

    for j0 in range(0, e, tn):
        js = slice(j0, j0 + tn)
        v_ref[:, js] = _gelu(_dot(h, wi_ref[:, e + j0:e + j0 + tn]))
        o_ref[:, js] = xr_ref[:, js] + gate[:, js] * _dot(s_prev, wo_ref[:, js])

    lng = lng_ref[...]
    lnb = lnb_ref[...]
    for r0 in range(0, tm, rows):
        vn_ref[r0:r0 + rows, :] = _layer_norm_rows(v_ref[r0:r0 + rows, :], lng, lnb).astype(BF16)

    for j0 in range(0, e, tn):
        u_ref[:, j0:j0 + tn] = _gelu(_dot(h, wi_ref[:, j0:j0 + tn]))

    for hd in range(n_heads):
        cs = slice(hd * dh, (hd + 1) * dh)
        bias = bs_ref[hd]
        for r0 in range(0, tm, length):
            rs = slice(r0, r0 + length)
            mixed = _dot(wsm_ref[hd], vn_ref[rs, cs]) + bias
            s_ref[rs, cs] = (u_ref[rs, cs] * mixed).astype(BF16)


def _gmlp_block(x2d, mods, norm_g3, w_in, ln_g, ln_b, ws, bs, w_out, layer, jm, seq, tm=256, rows=64, tn=256):
    m, d = x2d.shape
    e = w_out.shape[1]
    n_heads, length = ws.shape[1], ws.shape[2]
    assert length == CHUNK and seq % tm == 0 and tm % length == 0 and tm % rows == 0 and e % tn == 0
    assert (e // n_heads) % V7X_LANES == 0 and d == e
    n_blocks = m // tm
    kern = functools.partial(_gmlp_block_kernel, n_blocks=n_blocks, blocks_per_batch=seq // tm, rows=rows, tn=tn)
    vmem = _vmem_limit(
        pipelined=3 * _nbytes((tm, d), F32) + 6 * _nbytes((MOD_ROWS, d), F32)
        + 2 * _nbytes((n_heads, length, length), F32),
        single=_nbytes((d, 2 * e), BF16) + _nbytes((e, d), BF16),
        scratch=2 * _nbytes((tm, e), F32) + 2 * _nbytes((tm, e), BF16) + _nbytes((n_heads, length, length), BF16),
        temps=_nbytes((tm, d), BF16) + 3 * _nbytes((tm, d), F32))
    cur = lambda i: (jnp.minimum(i, n_blocks - 1), 0)
    lag = lambda i: (jnp.maximum(i - 1, 0), 0)
    return pl.pallas_call(
        kern,
        grid=(n_blocks + 1,),
        in_specs=[
            pl.BlockSpec((tm, d), cur),
            pl.BlockSpec((tm, d), lag),
            pl.BlockSpec((None, 1, d), lambda i: (layer * 3 + 1, 0, 0)),
            _mod_spec(layer, N_MOD + 0, d),
            _mod_spec(layer, N_MOD + 1, d),
            _mod_spec(layer, N_MOD + 2, d),
            pl.BlockSpec((None, d, 2 * e), lambda i: (jm, 0, 0), pipeline_mode=pl.Buffered(1)),
            pl.BlockSpec((None, 1, e), lambda i: (jm, 0, 0)),
            pl.BlockSpec((None, 1, e), lambda i: (jm, 0, 0)),
            pl.BlockSpec((None, n_heads, length, length), lambda i: (jm, 0, 0, 0)),
            pl.BlockSpec((None, n_heads, length, 1), lambda i: (jm, 0, 0, 0)),
            pl.BlockSpec((None, e, d), lambda i: (jm, 0, 0), pipeline_mode=pl.Buffered(1)),
        ],
        out_specs=pl.BlockSpec((tm, d), lag),
        out_shape=jax.ShapeDtypeStruct((m, d), F32),
        scratch_shapes=[pltpu.VMEM((tm, e), F32), pltpu.VMEM((tm, e), F32), pltpu.VMEM((tm, e), BF16),
                        pltpu.VMEM((tm, e), BF16), pltpu.VMEM((n_heads, length, length), BF16)],
        compiler_params=pltpu.CompilerParams(dimension_semantics=("arbitrary",), vmem_limit_bytes=vmem),
        name="gmlp_block",
    )(x2d, x2d, norm_g3, mods, mods, mods, w_in, ln_g[:, None, :], ln_b[:, None, :], ws, bs[..., None], w_out)


def kernel(x, c, ada_w, ada_b, norm_g, ffn_w_in, ffn_w_out, gm_w_in, gm_ln_g, gm_ln_b, gm_ws, gm_bs, gm_w_out,
           cv_w_in, cv_b_in, cv_dw_w, cv_dw_b, cv_ln_g, cv_ln_b, cv_w_out, cv_b_out, final_g):
    bsz, seq, d = x.shape
    depth, n_sub = norm_g.shape[0], norm_g.shape[1]
    n_mixers = 2
    assert n_sub == 3 and ada_w.shape[2] == n_sub * N_MOD * d

    mods = _ada_table(c, ada_w, ada_b)
    norm_g3 = norm_g.reshape(depth * n_sub, 1, d)
    gm_w_in, gm_w_out = gm_w_in.astype(BF16), gm_w_out.astype(BF16)
    cv_w_in, cv_w_out = cv_w_in.astype(BF16), cv_w_out.astype(BF16)

    h = x.reshape(bsz * seq, d)
    for i in range(depth):
        h = _ffn(h, mods, norm_g3, ffn_w_in, ffn_w_out, i, 0, 0, seq)
        jm = i // n_mixers
        if i % n_mixers == 0:
            h = _gmlp_block(h, mods, norm_g3, gm_w_in, gm_ln_g, gm_ln_b, gm_ws, gm_bs, gm_w_out, i, jm, seq)
        else:
            h = _conv_block(h, mods, norm_g3, cv_w_in, cv_b_in, cv_dw_w, cv_dw_b, cv_ln_g, cv_ln_b, cv_w_out,
                            cv_b_out, i, jm, seq)
        h = _ffn(h, mods, norm_g3, ffn_w_in, ffn_w_out, i, 2, 1, seq,
                 final_g=final_g if i == depth - 1 else None)
    return h.reshape(bsz, seq, d)
```

```python
import functools

import jax
import jax.numpy as jnp
from jax import lax
from jax.experimental import pallas as pl
from jax.experimental.pallas import tpu as pltpu

EPS = 1e-6
N_MOD = 3
CHUNK = 128
MOD_ROWS = 8
CONV_HALO = 32

V7X_VMEM_BYTES = 64 * 1024 * 1024
V7X_LANES = 128
SUBLANES = 8

F32 = jnp.float32
BF16 = jnp.bfloat16


def _vmem_limit(pipelined, single, scratch, temps):
    need = 2 * pipelined + single + scratch + temps
    cap = V7X_VMEM_BYTES - 1024 * 1024
    assert need <= cap, need
    return int(min(need + need // 10, cap))


def _nbytes(shape, dtype):
    n = 1
    for s in shape:
        n *= s
    return n * jnp.dtype(dtype).itemsize


def _dot(a, b):
    return jnp.dot(a, b, preferred_element_type=F32)


def _norm_mod_rows(x, g, shift, scale):
    ms = jnp.mean(x * x, axis=-1, keepdims=True)
    return x * lax.rsqrt(ms + EPS) * (g * (1.0 + scale)) + shift


def _ada_kernel(c_ref, w_ref, b_ref, o_ref):
    c = c_ref[...]
    cond = (c * jax.nn.sigmoid(c)).astype(BF16)
    o_ref[...] = _dot(cond, w_ref[...].astype(BF16)) + b_ref[...]


def _ada_table(c, ada_w, ada_b, tn=1024):
    depth, d, n = ada_w.shape
    bsz = c.shape[0]
    assert bsz <= MOD_ROWS and n % d == 0 and d % tn == 0
    per = d // tn
    c8 = jnp.pad(c, ((0, MOD_ROWS - bsz), (0, 0)))
    b4 = ada_b.reshape(depth, n // d, 1, d)
    return pl.pallas_call(
        _ada_kernel,
        grid=(depth, n // tn),
        in_specs=[
            pl.BlockSpec((MOD_ROWS, d), lambda l, j: (0, 0)),
            pl.BlockSpec((None, d, tn), lambda l, j: (l, 0, j)),
            pl.BlockSpec((None, None, 1, tn), lambda l, j: (l, j // per, 0, j % per)),
        ],
        out_specs=pl.BlockSpec((None, None, MOD_ROWS, tn), lambda l, j: (l, j // per, 0, j % per)),
        out_shape=jax.ShapeDtypeStruct((depth, n // d, MOD_ROWS, d), F32),
        compiler_params=pltpu.CompilerParams(
            dimension_semantics=("parallel", "parallel"),
            vmem_limit_bytes=_vmem_limit(
                _nbytes((d, tn), F32) + _nbytes((MOD_ROWS, d + 2 * tn), F32), 0, 0,
                2 * _nbytes((d, tn), BF16) + 4 * _nbytes((MOD_ROWS, d), F32))),
        name="ada",
    )(c8, ada_w, b4)


def _mod_spec(layer, row, d):
    return pl.BlockSpec((None, None, MOD_ROWS, d), lambda *_: (layer, row, 0, 0))


def _ffn_kernel(x_hbm, g_ref, shift_ref, scale_ref, gate_ref, wg_ref, wu_ref, wo_ref, *rest,
                blocks_per_batch, n_i, n_f, n_chunk, rows, final):
    if final:
        fg_ref, o_ref, h_ref, xbuf_ref, sem = rest
    else:
        o_ref, h_ref, xbuf_ref, sem = rest
    i = pl.program_id(0)
    f = pl.program_id(1)
    b = i // blocks_per_batch
    tm, d = o_ref.shape

    def x_copy(blk):
        return pltpu.make_async_copy(x_hbm.at[pl.ds(blk * tm, tm), :], xbuf_ref, sem.at[0])

    @pl.when(f == 0)
    def _():
        @pl.when(i == 0)
        def _():
            x_copy(i).start()

        x_copy(i).wait()
        g = g_ref[...]
        shift = shift_ref[pl.ds(b, 1), :]
        scale = scale_ref[pl.ds(b, 1), :]
        for r0 in range(0, tm, rows):
            rs = slice(r0, r0 + rows)
            x = xbuf_ref[rs, :]
            h_ref[rs, :] = _norm_mod_rows(x, g, shift, scale).astype(BF16)
            o_ref[rs, :] = x

    @pl.when((f == 1) & (i + 1 < n_i))
    def _():
        x_copy(i + 1).start()

    half_gate = 0.5 * gate_ref[pl.ds(b, 1), :]
    h = h_ref[...]
    gt = _dot(h, wg_ref[...].astype(BF16))
    up = _dot(h, wu_ref[...].astype(BF16))
    a = (gt * jax.nn.sigmoid(gt) * up).astype(BF16)
    for n0 in range(0, d, n_chunk):
        ns = slice(n0, n0 + n_chunk)
        o_ref[:, ns] += half_gate[:, ns] * _dot(a, wo_ref[:, ns].astype(BF16))

    if final:
        @pl.when(f == n_f - 1)
        def _():
            fg = fg_ref[...]
            for r0 in range(0, tm, rows):
                o = o_ref[r0:r0 + rows, :]
                ms = jnp.mean(o * o, axis=-1, keepdims=True)
                o_ref[r0:r0 + rows, :] = o * lax.rsqrt(ms + EPS) * fg


def _ffn(x2d, mods, norm_g3, w_in, w_out, layer, sub, which, seq, final_g=None, tm=1024, tf=512,
         n_chunk=512, rows=64):
    m, d = x2d.shape
    f_dim = w_out.shape[2]
    assert seq % tm == 0 and f_dim % tf == 0 and d % n_chunk == 0 and tm % rows == 0
    n_f = f_dim // tf
    assert n_f >= 2
    row0 = sub * N_MOD
    in_specs = [
        pl.BlockSpec(memory_space=pl.ANY),
        pl.BlockSpec((None, 1, d), lambda i, f: (layer * 3 + sub, 0, 0)),
        _mod_spec(layer, row0 + 0, d),
        _mod_spec(layer, row0 + 1, d),
        _mod_spec(layer, row0 + 2, d),
        pl.BlockSpec((None, None, d, tf), lambda i, f: (layer, which, 0, f)),
        pl.BlockSpec((None, None, d, tf), lambda i, f: (layer, which, 0, f + n_f)),
        pl.BlockSpec((None, None, tf, d), lambda i, f: (layer, which, f, 0)),
    ]
    args = [x2d, norm_g3, mods, mods, mods, w_in, w_in, w_out]
    if final_g is not None:
        in_specs.append(pl.BlockSpec((1, d), lambda i, f: (0, 0)))
        args.append(final_g.reshape(1, d))
    kern = functools.partial(_ffn_kernel, blocks_per_batch=seq // tm, n_i=m // tm, n_f=n_f, n_chunk=n_chunk,
                             rows=rows, final=final_g is not None)
    vmem = _vmem_limit(
        pipelined=_nbytes((tm, d), F32) + 3 * _nbytes((d, tf), w_in.dtype) + 5 * _nbytes((MOD_ROWS, d), F32),
        single=0,
        scratch=_nbytes((tm, d), BF16) + _nbytes((tm, d), F32),
        temps=2 * _nbytes((tm, tf), F32) + _nbytes((tm, tf), BF16))
    return pl.pallas_call(
        kern,
        grid=(m // tm, n_f),
        in_specs=in_specs,
        out_specs=pl.BlockSpec((tm, d), lambda i, f: (i, 0)),
        out_shape=jax.ShapeDtypeStruct((m, d), F32),
        scratch_shapes=[pltpu.VMEM((tm, d), BF16), pltpu.VMEM((tm, d), F32), pltpu.SemaphoreType.DMA((1,))],
        compiler_params=pltpu.CompilerParams(
            dimension_semantics=("arbitrary", "arbitrary"), vmem_limit_bytes=vmem),
        name="ffn",
    )(*args)


def _layer_norm_rows(v, g, b):
    mu = jnp.mean(v, axis=-1, keepdims=True)
    dv = v - mu
    var = jnp.mean(dv * dv, axis=-1, keepdims=True)
    return dv * lax.rsqrt(var + EPS) * g + b


def _conv_block_kernel(xc_ref, xr_ref, g_ref, shift_ref, scale_ref, gate_ref, wi_ref, bi_ref, dww_ref, dwb_ref,
                       lng_ref, lnb_ref, wo_ref, bo_ref, o_ref, win_ref, yc_ref, s_ref,
                       *, n_blocks, blocks_per_batch, rows, tn):
    step = pl.program_id(0)
    tm, c = yc_ref.shape
    d = o_ref.shape[1]
    k_taps = dww_ref.shape[0]
    base = CONV_HALO - (k_taps - 1)

    @pl.when(step == 0)
    def _():
        win_ref[0:CONV_HALO, :] = jnp.zeros((CONV_HALO, c), F32)
        s_ref[...] = jnp.zeros_like(s_ref)

    b_in = jnp.minimum(step, n_blocks - 1) // blocks_per_batch
    h = _norm_mod_rows(xc_ref[...], g_ref[...], shift_ref[pl.ds(b_in, 1), :],
                       scale_ref[pl.ds(b_in, 1), :]).astype(BF16)
    gate = gate_ref[pl.ds(jnp.maximum(step - 1, 0) // blocks_per_batch, 1), :]
    s_prev = s_ref[...]

    for j0 in range(0, c, tn):
        js = slice(j0, j0 + tn)
        a = _dot(h, wi_ref[:, js]) + bi_ref[:, js]
        gl = _dot(h, wi_ref[:, c + j0:c + j0 + tn]) + bi_ref[:, c + j0:c + j0 + tn]
        win_ref[CONV_HALO:, js] = a * jax.nn.sigmoid(gl)
        o_ref[:, js] = xr_ref[:, js] + gate[:, js] * (_dot(s_prev, wo_ref[:, js]) + bo_ref[:, js])
        for l0 in range(j0, j0 + tn, V7X_LANES):
            cs = slice(l0, l0 + V7X_LANES)
            for r0 in range(0, tm, rows):
                acc = jnp.broadcast_to(dwb_ref[:, cs], (rows, V7X_LANES))
                for p in range(SUBLANES):
                    part = None
                    for a8 in range(0, CONV_HALO + SUBLANES, SUBLANES):
                        k = a8 + p - base
                        if 0 <= k < k_taps:
                            n_load = rows + (SUBLANES if p else 0)
                            term = dww_ref[k:k + 1, cs] * win_ref[r0 + a8:r0 + a8 + n_load, cs]
                            part = term if part is None else part + term
                    acc = acc + part[p:p + rows]
                yc_ref[r0:r0 + rows, cs] = acc

    lng = lng_ref[...]
    lnb = lnb_ref[...]
    for r0 in range(0, tm, rows):
        yl = _layer_norm_rows(yc_ref[r0:r0 + rows, :], lng, lnb)
        s_ref[r0:r0 + rows, :] = (yl * jax.nn.sigmoid(yl)).astype(BF16)

    tail = win_ref[tm:tm + CONV_HALO, :]
    win_ref[0:CONV_HALO, :] = jnp.where((step + 1) % blocks_per_batch == 0, 0.0, tail)


def _conv_block(x2d, mods, norm_g3, w_in, b_in, dw_w, dw_b, ln_g, ln_b, w_out, b_out, layer, jm, seq,
                tm=256, rows=64, tn=256):
    m, d = x2d.shape
    c = w_out.shape[1]
    k_taps = dw_w.shape[1]
    assert k_taps - 1 <= CONV_HALO and seq % tm == 0 and tm % rows == 0 and c % tn == 0 and d == c
    n_blocks = m // tm
    kern = functools.partial(_conv_block_kernel, n_blocks=n_blocks, blocks_per_batch=seq // tm, rows=rows, tn=tn)
    vmem = _vmem_limit(
        pipelined=3 * _nbytes((tm, d), F32) + 8 * _nbytes((MOD_ROWS, d), F32) + _nbytes((CONV_HALO, c), F32)
        + _nbytes((MOD_ROWS, 2 * c), F32),
        single=_nbytes((d, 2 * c), BF16) + _nbytes((c, d), BF16),
        scratch=_nbytes((2 * tm + CONV_HALO, c), F32) + _nbytes((tm, c), BF16),
        temps=_nbytes((tm, d), BF16) + 3 * _nbytes((tm, d), F32))
    cur = lambda i: (jnp.minimum(i, n_blocks - 1), 0)
    lag = lambda i: (jnp.maximum(i - 1, 0), 0)
    return pl.pallas_call(
        kern,
        grid=(n_blocks + 1,),
        in_specs=[
            pl.BlockSpec((tm, d), cur),
            pl.BlockSpec((tm, d), lag),
            pl.BlockSpec((None, 1, d), lambda i: (layer * 3 + 1, 0, 0)),
            _mod_spec(layer, N_MOD + 0, d),
            _mod_spec(layer, N_MOD + 1, d),
            _mod_spec(layer, N_MOD + 2, d),
            pl.BlockSpec((None, d, 2 * c), lambda i: (jm, 0, 0), pipeline_mode=pl.Buffered(1)),
            pl.BlockSpec((None, 1, 2 * c), lambda i: (jm, 0, 0)),
            pl.BlockSpec((None, k_taps, c), lambda i: (jm, 0, 0)),
            pl.BlockSpec((None, 1, c), lambda i: (jm, 0, 0)),
            pl.BlockSpec((None, 1, c), lambda i: (jm, 0, 0)),
            pl.BlockSpec((None, 1, c), lambda i: (jm, 0, 0)),
            pl.BlockSpec((None, c, d), lambda i: (jm, 0, 0), pipeline_mode=pl.Buffered(1)),
            pl.BlockSpec((None, 1, d), lambda i: (jm, 0, 0)),
        ],
        out_specs=pl.BlockSpec((tm, d), lag),
        out_shape=jax.ShapeDtypeStruct((m, d), F32),
        scratch_shapes=[pltpu.VMEM((tm + CONV_HALO, c), F32), pltpu.VMEM((tm, c), F32), pltpu.VMEM((tm, c), BF16)],
        compiler_params=pltpu.CompilerParams(dimension_semantics=("arbitrary",), vmem_limit_bytes=vmem),
        name="conv_block",
    )(x2d, x2d, norm_g3, mods, mods, mods, w_in, b_in[:, None, :], dw_w, dw_b[:, None, :], ln_g[:, None, :],
      ln_b[:, None, :], w_out, b_out[:, None, :])


def _gelu(z):
    return 0.5 * z * (1.0 + lax.erf(z * (2.0 ** -0.5)))


def _gmlp_block_kernel(xc_ref, xr_ref, g_ref, shift_ref, scale_ref, gate_ref, wi_ref, lng_ref, lnb_ref, ws_ref,
                       bs_ref, wo_ref, o_ref, u_ref, v_ref, vn_ref, s_ref, wsm_ref,
                       *, n_blocks, blocks_per_batch, rows, tn):
    step = pl.program_id(0)
    tm, e = u_ref.shape
    n_heads, length, _ = ws_ref.shape
    dh = e // n_heads

    @pl.when(step == 0)
    def _():
        s_ref[...] = jnp.zeros_like(s_ref)
        causal = (lax.broadcasted_iota(jnp.int32, (length, length), 0)
                  >= lax.broadcasted_iota(jnp.int32, (length, length), 1))
        for hd in range(n_heads):
            wsm_ref[hd] = jnp.where(causal, ws_ref[hd], 0.0).astype(BF16)

    b_in = jnp.minimum(step, n_blocks - 1) // blocks_per_batch
    h = _norm_mod_rows(xc_ref[...], g_ref[...], shift_ref[pl.ds(b_in, 1), :],
                       scale_ref[pl.ds(b_in, 1), :]).astype(BF16)
    gate = gate_ref[pl.ds(jnp.maximum(step - 1, 0) // blocks_per_batch, 1), :]
    s_prev = s_ref[...]

    for j0 in range(0, e, tn):
        js = slice(j0, j0 + tn)
        v_ref[:, js] = _gelu(_dot(h, wi_ref[:, e + j0:e + j0 + tn]))
        o_ref[:, js] = xr_ref[:, js] + gate[:, js] * _dot(s_prev, wo_ref[:, js])

    lng = lng_ref[...]
    lnb = lnb_ref[...]
    for r0 in range(0, tm, rows):
        vn_ref[r0:r0 + rows, :] = _layer_norm_rows(v_ref[r0:r0 + rows, :], lng, lnb).astype(BF16)

    for j0 in range(0, e, tn):
        u_ref[:, j0:j0 + tn] = _gelu(_dot(h, wi_ref[:, j0:j0 + tn]))

    for hd in range(n_heads):
        cs = slice(hd * dh, (hd + 1) * dh)
        bias = bs_ref[hd]
        for r0 in range(0, tm, length):
            rs = slice(r0, r0 + length)
            mixed = _dot(wsm_ref[hd], vn_ref[rs, cs]) + bias
            s_ref[rs, cs] = (u_ref[rs, cs] * mixed).astype(BF16)


def _gmlp_block(x2d, mods, norm_g3, w_in, ln_g, ln_b, ws, bs, w_out, layer, jm, seq, tm=256, rows=64, tn=256):
    m, d = x2d.shape
    e = w_out.shape[1]
    n_heads, length = ws.shape[1], ws.shape[2]
    assert length == CHUNK and seq % tm == 0 and tm % length == 0 and tm % rows == 0 and e % tn == 0
    assert (e // n_heads) % V7X_LANES == 0 and d == e
    n_blocks = m // tm
    kern = functools.partial(_gmlp_block_kernel, n_blocks=n_blocks, blocks_per_batch=seq // tm, rows=rows, tn=tn)
    vmem = _vmem_limit(
        pipelined=3 * _nbytes((tm, d), F32) + 6 * _nbytes((MOD_ROWS, d), F32)
        + 2 * _nbytes((n_heads, length, length), F32),
        single=_nbytes((d, 2 * e), BF16) + _nbytes((e, d), BF16),
        scratch=2 * _nbytes((tm, e), F32) + 2 * _nbytes((tm, e), BF16) + _nbytes((n_heads, length, length), BF16),
        temps=_nbytes((tm, d), BF16) + 3 * _nbytes((tm, d), F32))
    cur = lambda i: (jnp.minimum(i, n_blocks - 1), 0)
    lag = lambda i: (jnp.maximum(i - 1, 0), 0)
    return pl.pallas_call(
        kern,
        grid=(n_blocks + 1,),
        in_specs=[
            pl.BlockSpec((tm, d), cur),
            pl.BlockSpec((tm, d), lag),
            pl.BlockSpec((None, 1, d), lambda i: (layer * 3 + 1, 0, 0)),
            _mod_spec(layer, N_MOD + 0, d),
            _mod_spec(layer, N_MOD + 1, d),
            _mod_spec(layer, N_MOD + 2, d),
            pl.BlockSpec((None, d, 2 * e), lambda i: (jm, 0, 0), pipeline_mode=pl.Buffered(1)),
            pl.BlockSpec((None, 1, e), lambda i: (jm, 0, 0)),
            pl.BlockSpec((None, 1, e), lambda i: (jm, 0, 0)),
            pl.BlockSpec((None, n_heads, length, length), lambda i: (jm, 0, 0, 0)),
            pl.BlockSpec((None, n_heads, length, 1), lambda i: (jm, 0, 0, 0)),
            pl.BlockSpec((None, e, d), lambda i: (jm, 0, 0), pipeline_mode=pl.Buffered(1)),
        ],
        out_specs=pl.BlockSpec((tm, d), lag),
        out_shape=jax.ShapeDtypeStruct((m, d), F32),
        scratch_shapes=[pltpu.VMEM((tm, e), F32), pltpu.VMEM((tm, e), F32), pltpu.VMEM((tm, e), BF16),
                        pltpu.VMEM((tm, e), BF16), pltpu.VMEM((n_heads, length, length), BF16)],
        compiler_params=pltpu.CompilerParams(dimension_semantics=("arbitrary",), vmem_limit_bytes=vmem),
        name="gmlp_block",
    )(x2d, x2d, norm_g3, mods, mods, mods, w_in, ln_g[:, None, :], ln_b[:, None, :], ws, bs[..., None], w_out)


def kernel(x, c, ada_w, ada_b, norm_g, ffn_w_in, ffn_w_out, gm_w_in, gm_ln_g, gm_ln_b, gm_ws, gm_bs, gm_w_out,
           cv_w_in, cv_b_in, cv_dw_w, cv_dw_b, cv_ln_g, cv_ln_b, cv_w_out, cv_b_out, final_g):
    bsz, seq, d = x.shape
    depth, n_sub = norm_g.shape[0], norm_g.shape[1]
    n_mixers = 2
    assert n_sub == 3 and ada_w.shape[2] == n_sub * N_MOD * d

    mods = _ada_table(c, ada_w, ada_b)
    norm_g3 = norm_g.reshape(depth * n_sub, 1, d)
    gm_w_in, gm_w_out = gm_w_in.astype(BF16), gm_w_out.astype(BF16)
    cv_w_in, cv_w_out = cv_w_in.astype(BF16), cv_w_out.astype(BF16)

    h = x.reshape(bsz * seq, d)
    for i in range(depth):
        h = _ffn(h, mods, norm_g3, ffn_w_in, ffn_w_out, i, 0, 0, seq)
        jm = i // n_mixers
        if i % n_mixers == 0:
            h = _gmlp_block(h, mods, norm_g3, gm_w_in, gm_ln_g, gm_ln_b, gm_ws, gm_bs, gm_w_out, i, jm, seq)
        else:
            h = _conv_block(h, mods, norm_g3, cv_w_in, cv_b_in, cv_dw_w, cv_dw_b, cv_ln_g, cv_ln_b, cv_w_out,
                            cv_b_out, i, jm, seq)
        h = _ffn(h, mods, norm_g3, ffn_w_in, ffn_w_out, i, 2, 1, seq,
                 final_g=final_g if i == depth - 1 else None)
    return h.reshape(bsz, seq, d)
```

```python
import functools

import jax
import jax.numpy as jnp
from jax import lax
from jax.experimental import pallas as pl
from jax.experimental.pallas import tpu as pltpu

EPS = 1e-6
N_MOD = 3
CHUNK = 128
MOD_ROWS = 8
CONV_HALO = 32

V7X_VMEM_BYTES = 64 * 1024 * 1024
V7X_LANES = 128
SUBLANES = 8

F32 = jnp.float32
BF16 = jnp.bfloat16


def _vmem_limit(pipelined, single, scratch, temps):
    need = 2 * pipelined + single + scratch + temps
    cap = V7X_VMEM_BYTES - 1024 * 1024
    assert need <= cap, need
    return int(min(need + need // 10, cap))


def _nbytes(shape, dtype):
    n = 1
    for s in shape:
        n *= s
    return n * jnp.dtype(dtype).itemsize


def _dot(a, b):
    return jnp.dot(a, b, preferred_element_type=F32)


def _norm_mod_rows(x, g, shift, scale):
    ms = jnp.mean(x * x, axis=-1, keepdims=True)
    return x * lax.rsqrt(ms + EPS) * (g * (1.0 + scale)) + shift


def _ada_kernel(c_ref, w_ref, b_ref, o_ref):
    c = c_ref[...]
    cond = (c * jax.nn.sigmoid(c)).astype(BF16)
    o_ref[...] = _dot(cond, w_ref[...].astype(BF16)) + b_ref[...]


def _ada_table(c, ada_w, ada_b, tn=1024):
    depth, d, n = ada_w.shape
    bsz = c.shape[0]
    assert bsz <= MOD_ROWS and n % d == 0 and d % tn == 0
    per = d // tn
    c8 = jnp.pad(c, ((0, MOD_ROWS - bsz), (0, 0)))
    b4 = ada_b.reshape(depth, n // d, 1, d)
    return pl.pallas_call(
        _ada_kernel,
        grid=(depth, n // tn),
        in_specs=[
            pl.BlockSpec((MOD_ROWS, d), lambda l, j: (0, 0)),
            pl.BlockSpec((None, d, tn), lambda l, j: (l, 0, j)),
            pl.BlockSpec((None, None, 1, tn), lambda l, j: (l, j // per, 0, j % per)),
        ],
        out_specs=pl.BlockSpec((None, None, MOD_ROWS, tn), lambda l, j: (l, j // per, 0, j % per)),
        out_shape=jax.ShapeDtypeStruct((depth, n // d, MOD_ROWS, d), F32),
        compiler_params=pltpu.CompilerParams(
            dimension_semantics=("parallel", "parallel"),
            vmem_limit_bytes=_vmem_limit(
                _nbytes((d, tn), F32) + _nbytes((MOD_ROWS, d + 2 * tn), F32), 0, 0,
                2 * _nbytes((d, tn), BF16) + 4 * _nbytes((MOD_ROWS, d), F32))),
        name="ada",
    )(c8, ada_w, b4)


def _mod_spec(layer, row, d):
    return pl.BlockSpec((None, None, MOD_ROWS, d), lambda *_: (layer, row, 0, 0))


def _ffn_kernel(x_hbm, g_ref, shift_ref, scale_ref, gate_ref, wg_ref, wu_ref, wo_ref, *rest,
                blocks_per_batch, n_i, n_f, n_chunk, rows, final, cast_cols):
    rest = list(rest)
    fg_ref = rest.pop(0) if final else None
    n_cast = len(cast_cols)
    cast_src = [rest.pop(0) for _ in range(n_cast)]
    o_ref = rest.pop(0)
    cast_dst = [rest.pop(0) for _ in range(n_cast)]
    h_ref, xbuf_ref, sem = rest
    i = pl.program_id(0)
    f = pl.program_id(1)
    b = i // blocks_per_batch
    tm, d = o_ref.shape

    def x_copy(blk):
        return pltpu.make_async_copy(x_hbm.at[pl.ds(blk * tm, tm), :], xbuf_ref, sem.at[0])

    @pl.when(f == 0)
    def _():
        @pl.when(i == 0)
        def _():
            x_copy(i).start()

        x_copy(i).wait()
        g = g_ref[...]
        shift = shift_ref[pl.ds(b, 1), :]
        scale = scale_ref[pl.ds(b, 1), :]
        for r0 in range(0, tm, rows):
            rs = slice(r0, r0 + rows)
            x = xbuf_ref[rs, :]
            h_ref[rs, :] = _norm_mod_rows(x, g, shift, scale).astype(BF16)
            o_ref[rs, :] = x

    @pl.when((f == 1) & (i + 1 < n_i))
    def _():
        x_copy(i + 1).start()

    for src_ref, dst_ref, n_cols in zip(cast_src, cast_dst, cast_cols):
        @pl.when(f < n_cols)
        def _(src_ref=src_ref, dst_ref=dst_ref):
            dst_ref[...] = src_ref[...].astype(BF16)

    half_gate = 0.5 * gate_ref[pl.ds(b, 1), :]
    h = h_ref[...]
    gt = _dot(h, wg_ref[...].astype(BF16))
    up = _dot(h, wu_ref[...].astype(BF16))
    a = (gt * jax.nn.sigmoid(gt) * up).astype(BF16)
    for n0 in range(0, d, n_chunk):
        ns = slice(n0, n0 + n_chunk)
        o_ref[:, ns] += half_gate[:, ns] * _dot(a, wo_ref[:, ns].astype(BF16))

    if final:
        @pl.when(f == n_f - 1)
        def _():
            fg = fg_ref[...]
            for r0 in range(0, tm, rows):
                o = o_ref[r0:r0 + rows, :]
                ms = jnp.mean(o * o, axis=-1, keepdims=True)
                o_ref[r0:r0 + rows, :] = o * lax.rsqrt(ms + EPS) * fg


def _ffn(x2d, mods, norm_g3, w_in, w_out, layer, sub, which, seq, final_g=None, casts=(), tm=1024, tf=512,
         n_chunk=512, rows=64):
    m, d = x2d.shape
    f_dim = w_out.shape[2]
    assert seq % tm == 0 and f_dim % tf == 0 and d % n_chunk == 0 and tm % rows == 0
    n_i, n_f = m // tm, f_dim // tf
    assert n_f >= 2
    row0 = sub * N_MOD
    in_specs = [
        pl.BlockSpec(memory_space=pl.ANY),
        pl.BlockSpec((None, 1, d), lambda i, f: (layer * 3 + sub, 0, 0)),
        _mod_spec(layer, row0 + 0, d),
        _mod_spec(layer, row0 + 1, d),
        _mod_spec(layer, row0 + 2, d),
        pl.BlockSpec((None, None, d, tf), lambda i, f: (layer, which, 0, f)),
        pl.BlockSpec((None, None, d, tf), lambda i, f: (layer, which, 0, f + n_f)),
        pl.BlockSpec((None, None, tf, d), lambda i, f: (layer, which, f, 0)),
    ]
    args = [x2d, norm_g3, mods, mods, mods, w_in, w_in, w_out]
    if final_g is not None:
        in_specs.append(pl.BlockSpec((1, d), lambda i, f: (0, 0)))
        args.append(final_g.reshape(1, d))
    out_specs = [pl.BlockSpec((tm, d), lambda i, f: (i, 0))]
    out_shape = [jax.ShapeDtypeStruct((m, d), F32)]
    cast_cols, cast_bytes = [], 0
    for arr, lead in casts:
        r, c = arr.shape[-2:]
        n_cols = max(k for k in range(1, n_f + 1) if c % (k * V7X_LANES) == 0)
        assert r % n_i == 0 and (r // n_i) % (2 * SUBLANES) == 0
        blk = (r // n_i, c // n_cols)
        in_specs.append(pl.BlockSpec((None,) * len(lead) + blk,
                                     lambda i, f, lead=lead, n_cols=n_cols: lead + (i, jnp.minimum(f, n_cols - 1))))
        args.append(arr)
        out_specs.append(pl.BlockSpec(blk, lambda i, f, n_cols=n_cols: (i, jnp.minimum(f, n_cols - 1))))
        out_shape.append(jax.ShapeDtypeStruct((r, c), BF16))
        cast_cols.append(n_cols)
        cast_bytes += _nbytes(blk, F32) + _nbytes(blk, BF16)
    kern = functools.partial(_ffn_kernel, blocks_per_batch=seq // tm, n_i=n_i, n_f=n_f, n_chunk=n_chunk,
                             rows=rows, final=final_g is not None, cast_cols=tuple(cast_cols))
    vmem = _vmem_limit(
        pipelined=_nbytes((tm, d), F32) + 3 * _nbytes((d, tf), w_in.dtype) + 5 * _nbytes((MOD_ROWS, d), F32)
        + cast_bytes,
        single=0,
        scratch=_nbytes((tm, d), BF16) + _nbytes((tm, d), F32),
        temps=2 * _nbytes((tm, tf), F32) + _nbytes((tm, tf), BF16))
    outs = pl.pallas_call(
        kern,
        grid=(n_i, n_f),
        in_specs=in_specs,
        out_specs=out_specs,
        out_shape=out_shape,
        scratch_shapes=[pltpu.VMEM((tm, d), BF16), pltpu.VMEM((tm, d), F32), pltpu.SemaphoreType.DMA((1,))],
        compiler_params=pltpu.CompilerParams(
            dimension_semantics=("arbitrary", "arbitrary"), vmem_limit_bytes=vmem),
        name="ffn",
    )(*args)
    return outs if casts else outs[0]


def _layer_norm_rows(v, g, b):
    mu = jnp.mean(v, axis=-1, keepdims=True)
    dv = v - mu
    var = jnp.mean(dv * dv, axis=-1, keepdims=True)
    return dv * lax.rsqrt(var + EPS) * g + b


def _conv_block_kernel(xc_ref, xr_ref, g_ref, shift_ref, scale_ref, gate_ref, wi_ref, bi_ref, dww_ref, dwb_ref,
                       lng_ref, lnb_ref, wo_ref, bo_ref, o_ref, win_ref, yc_ref, s_ref,
                       *, n_blocks, blocks_per_batch, rows, tn):
    step = pl.program_id(0)
    tm, c = yc_ref.shape
    d = o_ref.shape[1]
    k_taps = dww_ref.shape[0]
    base = CONV_HALO - (k_taps - 1)

    @pl.when(step == 0)
    def _():
        win_ref[0:CONV_HALO, :] = jnp.zeros((CONV_HALO, c), F32)
        s_ref[...] = jnp.zeros_like(s_ref)

    b_in = jnp.minimum(step, n_blocks - 1) // blocks_per_batch
    h = _norm_mod_rows(xc_ref[...], g_ref[...], shift_ref[pl.ds(b_in, 1), :],
                       scale_ref[pl.ds(b_in, 1), :]).astype(BF16)
    gate = gate_ref[pl.ds(jnp.maximum(step - 1, 0) // blocks_per_batch, 1), :]
    s_prev = s_ref[...]

    for j0 in range(0, c, tn):
        js = slice(j0, j0 + tn)
        a = _dot(h, wi_ref[:, js]) + bi_ref[:, js]
        gl = _dot(h, wi_ref[:, c + j0:c + j0 + tn]) + bi_ref[:, c + j0:c + j0 + tn]
        win_ref[CONV_HALO:, js] = a * jax.nn.sigmoid(gl)
        o_ref[:, js] = xr_ref[:, js] + gate[:, js] * (_dot(s_prev, wo_ref[:, js]) + bo_ref[:, js])
        for l0 in range(j0, j0 + tn, V7X_LANES):
            cs = slice(l0, l0 + V7X_LANES)
            for r0 in range(0, tm, rows):
                acc = jnp.broadcast_to(dwb_ref[:, cs], (rows, V7X_LANES))
                for p in range(SUBLANES):
                    part = None
                    for a8 in range(0, CONV_HALO + SUBLANES, SUBLANES):
                        k = a8 + p - base
                        if 0 <= k < k_taps:
                            n_load = rows + (SUBLANES if p else 0)
                            term = dww_ref[k:k + 1, cs] * win_ref[r0 + a8:r0 + a8 + n_load, cs]
                            part = term if part is None else part + term
                    acc = acc + part[p:p + rows]
                yc_ref[r0:r0 + rows, cs] = acc

    lng = lng_ref[...]
    lnb = lnb_ref[...]
    for r0 in range(0, tm, rows):
        yl = _layer_norm_rows(yc_ref[r0:r0 + rows, :], lng, lnb)
        s_ref[r0:r0 + rows, :] = (yl * jax.nn.sigmoid(yl)).astype(BF16)

    tail = win_ref[tm:tm + CONV_HALO, :]
    win_ref[0:CONV_HALO, :] = jnp.where((step + 1) % blocks_per_batch == 0, 0.0, tail)


def _conv_block(x2d, mods, norm_g3, w_in, b_in, dw_w, dw_b, ln_g, ln_b, w_out, b_out, layer, jm, seq,
                tm=256, rows=64, tn=256):
    m, d = x2d.shape
    c = w_out.shape[0]
    k_taps = dw_w.shape[1]
    assert k_taps - 1 <= CONV_HALO and seq % tm == 0 and tm % rows == 0 and c % tn == 0 and d == c
    n_blocks = m // tm
    kern = functools.partial(_conv_block_kernel, n_blocks=n_blocks, blocks_per_batch=seq // tm, rows=rows, tn=tn)
    vmem = _vmem_limit(
        pipelined=3 * _nbytes((tm, d), F32) + 8 * _nbytes((MOD_ROWS, d), F32) + _nbytes((CONV_HALO, c), F32)
        + _nbytes((MOD_ROWS, 2 * c), F32),
        single=_nbytes((d, 2 * c), BF16) + _nbytes((c, d), BF16),
        scratch=_nbytes((2 * tm + CONV_HALO, c), F32) + _nbytes((tm, c), BF16),
        temps=_nbytes((tm, d), BF16) + 3 * _nbytes((tm, d), F32))
    cur = lambda i: (jnp.minimum(i, n_blocks - 1), 0)
    lag = lambda i: (jnp.maximum(i - 1, 0), 0)
    return pl.pallas_call(
        kern,
        grid=(n_blocks + 1,),
        in_specs=[
            pl.BlockSpec((tm, d), cur),
            pl.BlockSpec((tm, d), lag),
            pl.BlockSpec((None, 1, d), lambda i: (layer * 3 + 1, 0, 0)),
            _mod_spec(layer, N_MOD + 0, d),
            _mod_spec(layer, N_MOD + 1, d),
            _mod_spec(layer, N_MOD + 2, d),
            pl.BlockSpec((d, 2 * c), lambda i: (0, 0), pipeline_mode=pl.Buffered(1)),
            pl.BlockSpec((None, 1, 2 * c), lambda i: (jm, 0, 0)),
            pl.BlockSpec((None, k_taps, c), lambda i: (jm, 0, 0)),
            pl.BlockSpec((None, 1, c), lambda i: (jm, 0, 0)),
            pl.BlockSpec((None, 1, c), lambda i: (jm, 0, 0)),
            pl.BlockSpec((None, 1, c), lambda i: (jm, 0, 0)),
            pl.BlockSpec((c, d), lambda i: (0, 0), pipeline_mode=pl.Buffered(1)),
            pl.BlockSpec((None, 1, d), lambda i: (jm, 0, 0)),
        ],
        out_specs=pl.BlockSpec((tm, d), lag),
        out_shape=jax.ShapeDtypeStruct((m, d), F32),
        scratch_shapes=[pltpu.VMEM((tm + CONV_HALO, c), F32), pltpu.VMEM((tm, c), F32), pltpu.VMEM((tm, c), BF16)],
        compiler_params=pltpu.CompilerParams(dimension_semantics=("arbitrary",), vmem_limit_bytes=vmem),
        name="conv_block",
    )(x2d, x2d, norm_g3, mods, mods, mods, w_in, b_in[:, None, :], dw_w, dw_b[:, None, :], ln_g[:, None, :],
      ln_b[:, None, :], w_out, b_out[:, None, :])


def _gelu(z):
    return 0.5 * z * (1.0 + lax.erf(z * (2.0 ** -0.5)))


def _gmlp_block_kernel(xc_ref, xr_ref, g_ref, shift_ref, scale_ref, gate_ref, wi_ref, lng_ref, lnb_ref, ws_ref,
                       bs_ref, wo_ref, o_ref, u_ref, v_ref, vn_ref, s_ref, wsm_ref,
                       *, n_blocks, blocks_per_batch, rows, tn):
    step = pl.program_id(0)
    tm, e = u_ref.shape
    n_heads, length, _ = ws_ref.shape
    dh = e // n_heads

    @pl.when(step == 0)
    def _():
        s_ref[...] = jnp.zeros_like(s_ref)
        causal = (lax.broadcasted_iota(jnp.int32, (length, length), 0)
                  >= lax.broadcasted_iota(jnp.int32, (length, length), 1))
        for hd in range(n_heads):
            wsm_ref[hd] = jnp.where(causal, ws_ref[hd], 0.0).astype(BF16)

    b_in = jnp.minimum(step, n_blocks - 1) // blocks_per_batch
    h = _norm_mod_rows(xc_ref[...], g_ref[...], shift_ref[pl.ds(b_in, 1), :],
                       scale_ref[pl.ds(b_in, 1), :]).astype(BF16)
    gate = gate_ref[pl.ds(jnp.maximum(step - 1, 0) // blocks_per_batch, 1), :]
    s_prev = s_ref[...]

    for j0 in range(0, e, tn):
        js = slice(j0, j0 + tn)
        v_ref[:, js] = _gelu(_dot(h, wi_ref[:, e + j0:e + j0 + tn]))
        o_ref[:, js] = xr_ref[:, js] + gate[:, js] * _dot(s_prev, wo_ref[:, js])

    lng = lng_ref[...]
    lnb = lnb_ref[...]
    for r0 in range(0, tm, rows):
        vn_ref[r0:r0 + rows, :] = _layer_norm_rows(v_ref[r0:r0 + rows, :], lng, lnb).astype(BF16)

    for j0 in range(0, e, tn):
        u_ref[:, j0:j0 + tn] = _gelu(_dot(h, wi_ref[:, j0:j0 + tn]))

    for hd in range(n_heads):
        cs = slice(hd * dh, (hd + 1) * dh)
        bias = bs_ref[hd]
        for r0 in range(0, tm, length):
            rs = slice(r0, r0 + length)
            mixed = _dot(wsm_ref[hd], vn_ref[rs, cs]) + bias
            s_ref[rs, cs] = (u_ref[rs, cs] * mixed).astype(BF16)


def _gmlp_block(x2d, mods, norm_g3, w_in, ln_g, ln_b, ws, bs, w_out, layer, jm, seq, tm=256, rows=64, tn=256):
    m, d = x2d.shape
    e = w_out.shape[0]
    n_heads, length = ws.shape[1], ws.shape[2]
    assert length == CHUNK and seq % tm == 0 and tm % length == 0 and tm % rows == 0 and e % tn == 0
    assert (e // n_heads) % V7X_LANES == 0 and d == e
    n_blocks = m // tm
    kern = functools.partial(_gmlp_block_kernel, n_blocks=n_blocks, blocks_per_batch=seq // tm, rows=rows, tn=tn)
    vmem = _vmem_limit(
        pipelined=3 * _nbytes((tm, d), F32) + 6 * _nbytes((MOD_ROWS, d), F32)
        + 2 * _nbytes((n_heads, length, length), F32),
        single=_nbytes((d, 2 * e), BF16) + _nbytes((e, d), BF16),
        scratch=2 * _nbytes((tm, e), F32) + 2 * _nbytes((tm, e), BF16) + _nbytes((n_heads, length, length), BF16),
        temps=_nbytes((tm, d), BF16) + 3 * _nbytes((tm, d), F32))
    cur = lambda i: (jnp.minimum(i, n_blocks - 1), 0)
    lag = lambda i: (jnp.maximum(i - 1, 0), 0)
    return pl.pallas_call(
        kern,
        grid=(n_blocks + 1,),
        in_specs=[
            pl.BlockSpec((tm, d), cur),
            pl.BlockSpec((tm, d), lag),
            pl.BlockSpec((None, 1, d), lambda i: (layer * 3 + 1, 0, 0)),
            _mod_spec(layer, N_MOD + 0, d),
            _mod_spec(layer, N_MOD + 1, d),
            _mod_spec(layer, N_MOD + 2, d),
            pl.BlockSpec((d, 2 * e), lambda i: (0, 0), pipeline_mode=pl.Buffered(1)),
            pl.BlockSpec((None, 1, e), lambda i: (jm, 0, 0)),
            pl.BlockSpec((None, 1, e), lambda i: (jm, 0, 0)),
            pl.BlockSpec((None, n_heads, length, length), lambda i: (jm, 0, 0, 0)),
            pl.BlockSpec((None, n_heads, length, 1), lambda i: (jm, 0, 0, 0)),
            pl.BlockSpec((e, d), lambda i: (0, 0), pipeline_mode=pl.Buffered(1)),
        ],
        out_specs=pl.BlockSpec((tm, d), lag),
        out_shape=jax.ShapeDtypeStruct((m, d), F32),
        scratch_shapes=[pltpu.VMEM((tm, e), F32), pltpu.VMEM((tm, e), F32), pltpu.VMEM((tm, e), BF16),
                        pltpu.VMEM((tm, e), BF16), pltpu.VMEM((n_heads, length, length), BF16)],
        compiler_params=pltpu.CompilerParams(dimension_semantics=("arbitrary",), vmem_limit_bytes=vmem),
        name="gmlp_block",
    )(x2d, x2d, norm_g3, mods, mods, mods, w_in, ln_g[:, None, :], ln_b[:, None, :], ws, bs[..., None], w_out)


def kernel(x, c, ada_w, ada_b, norm_g, ffn_w_in, ffn_w_out, gm_w_in, gm_ln_g, gm_ln_b, gm_ws, gm_bs, gm_w_out,
           cv_w_in, cv_b_in, cv_dw_w, cv_dw_b, cv_ln_g, cv_ln_b, cv_w_out, cv_b_out, final_g):
    bsz, seq, d = x.shape
    depth, n_sub = norm_g.shape[0], norm_g.shape[1]
    n_mixers = 2
    assert n_sub == 3 and ada_w.shape[2] == n_sub * N_MOD * d

    mods = _ada_table(c, ada_w, ada_b)
    norm_g3 = norm_g.reshape(depth * n_sub, 1, d)

    h = x.reshape(bsz * seq, d)
    for i in range(depth):
        jm = i // n_mixers
        is_gmlp = i % n_mixers == 0
        mix_in, mix_out = (gm_w_in, gm_w_out) if is_gmlp else (cv_w_in, cv_w_out)
        h, w_in_b, w_out_b = _ffn(h, mods, norm_g3, ffn_w_in, ffn_w_out, i, 0, 0, seq,
                                  casts=((mix_in, (jm,)), (mix_out, (jm,))))
        if is_gmlp:
            h = _gmlp_block(h, mods, norm_g3, w_in_b, gm_ln_g, gm_ln_b, gm_ws, gm_bs, w_out_b, i, jm, seq)
        else:
            h = _conv_block(h, mods, norm_g3, w_in_b, cv_b_in, cv_dw_w, cv_dw_b, cv_ln_g, cv_ln_b, w_out_b,
                            cv_b_out, i, jm, seq)
        h = _ffn(h, mods, norm_g3, ffn_w_in, ffn_w_out, i, 2, 1, seq,
                 final_g=final_g if i == depth - 1 else None)
    return h.reshape(bsz, seq, d)
```

```python
import functools

import jax
import jax.numpy as jnp
from jax import lax
from jax.experimental import pallas as pl
from jax.experimental.pallas import tpu as pltpu

EPS = 1e-6
N_MOD = 3
CHUNK = 128
MOD_ROWS = 8
CONV_HALO = 32

V7X_VMEM_BYTES = 64 * 1024 * 1024
V7X_LANES = 128
SUBLANES = 8

F32 = jnp.float32
BF16 = jnp.bfloat16


def _vmem_limit(pipelined, single, scratch, temps):
    need = 2 * pipelined + single + scratch + temps
    cap = V7X_VMEM_BYTES - 1024 * 1024
    assert need <= cap, need
    return int(min(need + need // 10, cap))


def _nbytes(shape, dtype):
    n = 1
    for s in shape:
        n *= s
    return n * jnp.dtype(dtype).itemsize


def _dot(a, b):
    return jnp.dot(a, b, preferred_element_type=F32)


def _norm_mod_rows(x, g, shift, scale):
    ms = jnp.mean(x * x, axis=-1, keepdims=True)
    return x * lax.rsqrt(ms + EPS) * (g * (1.0 + scale)) + shift


def _ada_kernel(c_ref, w_ref, b_ref, o_ref):
    c = c_ref[...]
    cond = (c * jax.nn.sigmoid(c)).astype(BF16)
    o_ref[...] = _dot(cond, w_ref[...].astype(BF16)) + b_ref[...]


def _ada_table(c, ada_w, ada_b, tn=1024):
    depth, d, n = ada_w.shape
    bsz = c.shape[0]
    assert bsz <= MOD_ROWS and n % d == 0 and d % tn == 0
    per = d // tn
    c8 = jnp.pad(c, ((0, MOD_ROWS - bsz), (0, 0)))
    b4 = ada_b.reshape(depth, n // d, 1, d)
    return pl.pallas_call(
        _ada_kernel,
        grid=(depth, n // tn),
        in_specs=[
            pl.BlockSpec((MOD_ROWS, d), lambda l, j: (0, 0)),
            pl.BlockSpec((None, d, tn), lambda l, j: (l, 0, j)),
            pl.BlockSpec((None, None, 1, tn), lambda l, j: (l, j // per, 0, j % per)),
        ],
        out_specs=pl.BlockSpec((None, None, MOD_ROWS, tn), lambda l, j: (l, j // per, 0, j % per)),
        out_shape=jax.ShapeDtypeStruct((depth, n // d, MOD_ROWS, d), F32),
        compiler_params=pltpu.CompilerParams(
            dimension_semantics=("parallel", "parallel"),
            vmem_limit_bytes=_vmem_limit(
                _nbytes((d, tn), F32) + _nbytes((MOD_ROWS, d + 2 * tn), F32), 0, 0,
                2 * _nbytes((d, tn), BF16) + 4 * _nbytes((MOD_ROWS, d), F32))),
        name="ada",
    )(c8, ada_w, b4)


def _mod_spec(layer, row, d):
    return pl.BlockSpec((None, None, MOD_ROWS, d), lambda *_: (layer, row, 0, 0))


def _ffn_kernel(x_hbm, g_ref, shift_ref, scale_ref, gate_ref, wg_ref, wu_ref, wo_ref, *rest,
                blocks_per_batch, n_i, n_f, n_chunk, rows, final, cast_cols):
    rest = list(rest)
    fg_ref = rest.pop(0) if final else None
    n_cast = len(cast_cols)
    cast_src = [rest.pop(0) for _ in range(n_cast)]
    o_ref = rest.pop(0)
    cast_dst = [rest.pop(0) for _ in range(n_cast)]
    h_ref, xbuf_ref, sem = rest
    i = pl.program_id(0)
    f = pl.program_id(1)
    b = i // blocks_per_batch
    tm, d = o_ref.shape

    def x_copy(blk):
        return pltpu.make_async_copy(x_hbm.at[pl.ds(blk * tm, tm), :], xbuf_ref, sem.at[0])

    @pl.when(f == 0)
    def _():
        @pl.when(i == 0)
        def _():
            x_copy(i).start()

        x_copy(i).wait()
        g = g_ref[...]
        shift = shift_ref[pl.ds(b, 1), :]
        scale = scale_ref[pl.ds(b, 1), :]
        for r0 in range(0, tm, rows):
            rs = slice(r0, r0 + rows)
            x = xbuf_ref[rs, :]
            h_ref[rs, :] = _norm_mod_rows(x, g, shift, scale).astype(BF16)
            o_ref[rs, :] = x

    @pl.when((f == n_f // 2) & (i + 1 < n_i))
    def _():
        x_copy(i + 1).start(priority=1)

    for src_ref, dst_ref, n_cols in zip(cast_src, cast_dst, cast_cols):
        @pl.when(f < n_cols)
        def _(src_ref=src_ref, dst_ref=dst_ref):
            dst_ref[...] = src_ref[...].astype(BF16)

    half_gate = 0.5 * gate_ref[pl.ds(b, 1), :]
    h = h_ref[...]
    gt = _dot(h, wg_ref[...].astype(BF16))
    up = _dot(h, wu_ref[...].astype(BF16))
    a = (gt * jax.nn.sigmoid(gt) * up).astype(BF16)
    for n0 in range(0, d, n_chunk):
        ns = slice(n0, n0 + n_chunk)
        o_ref[:, ns] += half_gate[:, ns] * _dot(a, wo_ref[:, ns].astype(BF16))

    if final:
        @pl.when(f == n_f - 1)
        def _():
            fg = fg_ref[...]
            for r0 in range(0, tm, rows):
                o = o_ref[r0:r0 + rows, :]
                ms = jnp.mean(o * o, axis=-1, keepdims=True)
                o_ref[r0:r0 + rows, :] = o * lax.rsqrt(ms + EPS) * fg


def _ffn(x2d, mods, norm_g3, w_in, w_out, layer, sub, which, seq, final_g=None, casts=(), tm=1024, tf=512,
         n_chunk=512, rows=64):
    m, d = x2d.shape
    f_dim = w_out.shape[2]
    assert seq % tm == 0 and f_dim % tf == 0 and d % n_chunk == 0 and tm % rows == 0
    n_i, n_f = m // tm, f_dim // tf
    assert n_f >= 2
    row0 = sub * N_MOD
    in_specs = [
        pl.BlockSpec(memory_space=pl.ANY),
        pl.BlockSpec((None, 1, d), lambda i, f: (layer * 3 + sub, 0, 0)),
        _mod_spec(layer, row0 + 0, d),
        _mod_spec(layer, row0 + 1, d),
        _mod_spec(layer, row0 + 2, d),
        pl.BlockSpec((None, None, d, tf), lambda i, f: (layer, which, 0, f)),
        pl.BlockSpec((None, None, d, tf), lambda i, f: (layer, which, 0, f + n_f)),
        pl.BlockSpec((None, None, tf, d), lambda i, f: (layer, which, f, 0)),
    ]
    args = [x2d, norm_g3, mods, mods, mods, w_in, w_in, w_out]
    if final_g is not None:
        in_specs.append(pl.BlockSpec((1, d), lambda i, f: (0, 0)))
        args.append(final_g.reshape(1, d))
    out_specs = [pl.BlockSpec((tm, d), lambda i, f: (i, 0))]
    out_shape = [jax.ShapeDtypeStruct((m, d), F32)]
    cast_cols, cast_bytes = [], 0
    for arr, lead in casts:
        r, c = arr.shape[-2:]
        n_cols = max(k for k in range(1, n_f + 1) if c % (k * V7X_LANES) == 0)
        assert r % n_i == 0 and (r // n_i) % (2 * SUBLANES) == 0
        blk = (r // n_i, c // n_cols)
        in_specs.append(pl.BlockSpec((None,) * len(lead) + blk,
                                     lambda i, f, lead=lead, n_cols=n_cols: lead + (i, jnp.minimum(f, n_cols - 1))))
        args.append(arr)
        out_specs.append(pl.BlockSpec(blk, lambda i, f, n_cols=n_cols: (i, jnp.minimum(f, n_cols - 1))))
        out_shape.append(jax.ShapeDtypeStruct((r, c), BF16))
        cast_cols.append(n_cols)
        cast_bytes += _nbytes(blk, F32) + _nbytes(blk, BF16)
    kern = functools.partial(_ffn_kernel, blocks_per_batch=seq // tm, n_i=n_i, n_f=n_f, n_chunk=n_chunk,
                             rows=rows, final=final_g is not None, cast_cols=tuple(cast_cols))
    vmem = _vmem_limit(
        pipelined=_nbytes((tm, d), F32) + 3 * _nbytes((d, tf), w_in.dtype) + 5 * _nbytes((MOD_ROWS, d), F32)
        + cast_bytes,
        single=0,
        scratch=_nbytes((tm, d), BF16) + _nbytes((tm, d), F32),
        temps=2 * _nbytes((tm, tf), F32) + _nbytes((tm, tf), BF16))
    outs = pl.pallas_call(
        kern,
        grid=(n_i, n_f),
        in_specs=in_specs,
        out_specs=out_specs,
        out_shape=out_shape,
        scratch_shapes=[pltpu.VMEM((tm, d), BF16), pltpu.VMEM((tm, d), F32), pltpu.SemaphoreType.DMA((1,))],
        compiler_params=pltpu.CompilerParams(
            dimension_semantics=("arbitrary", "arbitrary"), vmem_limit_bytes=vmem),
        name="ffn",
    )(*args)
    return outs if casts else outs[0]


def _layer_norm_rows(v, g, b):
    mu = jnp.mean(v, axis=-1, keepdims=True)
    dv = v - mu
    var = jnp.mean(dv * dv, axis=-1, keepdims=True)
    return dv * lax.rsqrt(var + EPS) * g + b


def _conv_block_kernel(xc_ref, xr_ref, g_ref, shift_ref, scale_ref, gate_ref, wi_ref, bi_ref, dww_ref, dwb_ref,
                       lng_ref, lnb_ref, wo_ref, bo_ref, o_ref, win_ref, yc_ref, s_ref,
                       *, n_blocks, blocks_per_batch, rows, tn):
    step = pl.program_id(0)
    tm, c = yc_ref.shape
    d = o_ref.shape[1]
    k_taps = dww_ref.shape[0]
    base = CONV_HALO - (k_taps - 1)

    @pl.when(step == 0)
    def _():
        win_ref[0:CONV_HALO, :] = jnp.zeros((CONV_HALO, c), F32)
        s_ref[...] = jnp.zeros_like(s_ref)

    b_in = jnp.minimum(step, n_blocks - 1) // blocks_per_batch
    h = _norm_mod_rows(xc_ref[...], g_ref[...], shift_ref[pl.ds(b_in, 1), :],
                       scale_ref[pl.ds(b_in, 1), :]).astype(BF16)
    gate = gate_ref[pl.ds(jnp.maximum(step - 1, 0) // blocks_per_batch, 1), :]
    s_prev = s_ref[...]

    for j0 in range(0, c, tn):
        js = slice(j0, j0 + tn)
        a = _dot(h, wi_ref[:, js]) + bi_ref[:, js]
        gl = _dot(h, wi_ref[:, c + j0:c + j0 + tn]) + bi_ref[:, c + j0:c + j0 + tn]
        win_ref[CONV_HALO:, js] = a * jax.nn.sigmoid(gl)
        o_ref[:, js] = xr_ref[:, js] + gate[:, js] * (_dot(s_prev, wo_ref[:, js]) + bo_ref[:, js])
        for l0 in range(j0, j0 + tn, V7X_LANES):
            cs = slice(l0, l0 + V7X_LANES)
            for r0 in range(0, tm, rows):
                acc = jnp.broadcast_to(dwb_ref[:, cs], (rows, V7X_LANES))
                for p in range(SUBLANES):
                    part = None
                    for a8 in range(0, CONV_HALO + SUBLANES, SUBLANES):
                        k = a8 + p - base
                        if 0 <= k < k_taps:
                            n_load = rows + (SUBLANES if p else 0)
                            term = dww_ref[k:k + 1, cs] * win_ref[r0 + a8:r0 + a8 + n_load, cs]
                            part = term if part is None else part + term
                    acc = acc + part[p:p + rows]
                yc_ref[r0:r0 + rows, cs] = acc

    lng = lng_ref[...]
    lnb = lnb_ref[...]
    for r0 in range(0, tm, rows):
        yl = _layer_norm_rows(yc_ref[r0:r0 + rows, :], lng, lnb)
        s_ref[r0:r0 + rows, :] = (yl * jax.nn.sigmoid(yl)).astype(BF16)

    tail = win_ref[tm:tm + CONV_HALO, :]
    win_ref[0:CONV_HALO, :] = jnp.where((step + 1) % blocks_per_batch == 0, 0.0, tail)


def _conv_block(x2d, mods, norm_g3, w_in, b_in, dw_w, dw_b, ln_g, ln_b, w_out, b_out, layer, jm, seq,
                tm=256, rows=64, tn=256):
    m, d = x2d.shape
    c = w_out.shape[0]
    k_taps = dw_w.shape[1]
    assert k_taps - 1 <= CONV_HALO and seq % tm == 0 and tm % rows == 0 and c % tn == 0 and d == c
    n_blocks = m // tm
    kern = functools.partial(_conv_block_kernel, n_blocks=n_blocks, blocks_per_batch=seq // tm, rows=rows, tn=tn)
    vmem = _vmem_limit(
        pipelined=3 * _nbytes((tm, d), F32) + 8 * _nbytes((MOD_ROWS, d), F32) + _nbytes((CONV_HALO, c), F32)
        + _nbytes((MOD_ROWS, 2 * c), F32),
        single=_nbytes((d, 2 * c), BF16) + _nbytes((c, d), BF16),
        scratch=_nbytes((2 * tm + CONV_HALO, c), F32) + _nbytes((tm, c), BF16),
        temps=_nbytes((tm, d), BF16) + 3 * _nbytes((tm, d), F32))
    cur = lambda i: (jnp.minimum(i, n_blocks - 1), 0)
    lag = lambda i: (jnp.maximum(i - 1, 0), 0)
    return pl.pallas_call(
        kern,
        grid=(n_blocks + 1,),
        in_specs=[
            pl.BlockSpec((tm, d), cur),
            pl.BlockSpec((tm, d), lag),
            pl.BlockSpec((None, 1, d), lambda i: (layer * 3 + 1, 0, 0)),
            _mod_spec(layer, N_MOD + 0, d),
            _mod_spec(layer, N_MOD + 1, d),
            _mod_spec(layer, N_MOD + 2, d),
            pl.BlockSpec((d, 2 * c), lambda i: (0, 0), pipeline_mode=pl.Buffered(1)),
            pl.BlockSpec((None, 1, 2 * c), lambda i: (jm, 0, 0)),
            pl.BlockSpec((None, k_taps, c), lambda i: (jm, 0, 0)),
            pl.BlockSpec((None, 1, c), lambda i: (jm, 0, 0)),
            pl.BlockSpec((None, 1, c), lambda i: (jm, 0, 0)),
            pl.BlockSpec((None, 1, c), lambda i: (jm, 0, 0)),
            pl.BlockSpec((c, d), lambda i: (0, 0), pipeline_mode=pl.Buffered(1)),
            pl.BlockSpec((None, 1, d), lambda i: (jm, 0, 0)),
        ],
        out_specs=pl.BlockSpec((tm, d), lag),
        out_shape=jax.ShapeDtypeStruct((m, d), F32),
        scratch_shapes=[pltpu.VMEM((tm + CONV_HALO, c), F32), pltpu.VMEM((tm, c), F32), pltpu.VMEM((tm, c), BF16)],
        compiler_params=pltpu.CompilerParams(dimension_semantics=("arbitrary",), vmem_limit_bytes=vmem),
        name="conv_block",
    )(x2d, x2d, norm_g3, mods, mods, mods, w_in, b_in[:, None, :], dw_w, dw_b[:, None, :], ln_g[:, None, :],
      ln_b[:, None, :], w_out, b_out[:, None, :])


def _gelu(z):
    return 0.5 * z * (1.0 + lax.erf(z * (2.0 ** -0.5)))


def _gmlp_block_kernel(xc_ref, xr_ref, g_ref, shift_ref, scale_ref, gate_ref, wi_ref, lng_ref, lnb_ref, ws_ref,
                       bs_ref, wo_ref, o_ref, u_ref, v_ref, vn_ref, s_ref, wsm_ref,
                       *, n_blocks, blocks_per_batch, rows, tn):
    step = pl.program_id(0)
    tm, e = u_ref.shape
    n_heads, length, _ = ws_ref.shape
    dh = e // n_heads

    @pl.when(step == 0)
    def _():
        s_ref[...] = jnp.zeros_like(s_ref)
        causal = (lax.broadcasted_iota(jnp.int32, (length, length), 0)
                  >= lax.broadcasted_iota(jnp.int32, (length, length), 1))
        for hd in range(n_heads):
            wsm_ref[hd] = jnp.where(causal, ws_ref[hd], 0.0).astype(BF16)

    b_in = jnp.minimum(step, n_blocks - 1) // blocks_per_batch
    h = _norm_mod_rows(xc_ref[...], g_ref[...], shift_ref[pl.ds(b_in, 1), :],
                       scale_ref[pl.ds(b_in, 1), :]).astype(BF16)
    gate = gate_ref[pl.ds(jnp.maximum(step - 1, 0) // blocks_per_batch, 1), :]
    s_prev = s_ref[...]

    for j0 in range(0, e, tn):
        js = slice(j0, j0 + tn)
        v_ref[:, js] = _gelu(_dot(h, wi_ref[:, e + j0:e + j0 + tn]))
        o_ref[:, js] = xr_ref[:, js] + gate[:, js] * _dot(s_prev, wo_ref[:, js])

    lng = lng_ref[...]
    lnb = lnb_ref[...]
    for r0 in range(0, tm, rows):
        vn_ref[r0:r0 + rows, :] = _layer_norm_rows(v_ref[r0:r0 + rows, :], lng, lnb).astype(BF16)

    for j0 in range(0, e, tn):
        u_ref[:, j0:j0 + tn] = _gelu(_dot(h, wi_ref[:, j0:j0 + tn]))

    for hd in range(n_heads):
        cs = slice(hd * dh, (hd + 1) * dh)
        bias = bs_ref[hd]
        for r0 in range(0, tm, length):
            rs = slice(r0, r0 + length)
            mixed = _dot(wsm_ref[hd], vn_ref[rs, cs]) + bias
            s_ref[rs, cs] = (u_ref[rs, cs] * mixed).astype(BF16)


def _gmlp_block(x2d, mods, norm_g3, w_in, ln_g, ln_b, ws, bs, w_out, layer, jm, seq, tm=256, rows=64, tn=256):
    m, d = x2d.shape
    e = w_out.shape[0]
    n_heads, length = ws.shape[1], ws.shape[2]
    assert length == CHUNK and seq % tm == 0 and tm % length == 0 and tm % rows == 0 and e % tn == 0
    assert (e // n_heads) % V7X_LANES == 0 and d == e
    n_blocks = m // tm
    kern = functools.partial(_gmlp_block_kernel, n_blocks=n_blocks, blocks_per_batch=seq // tm, rows=rows, tn=tn)
    vmem = _vmem_limit(
        pipelined=3 * _nbytes((tm, d), F32) + 6 * _nbytes((MOD_ROWS, d), F32)
        + 2 * _nbytes((n_heads, length, length), F32),
        single=_nbytes((d, 2 * e), BF16) + _nbytes((e, d), BF16),
        scratch=2 * _nbytes((tm, e), F32) + 2 * _nbytes((tm, e), BF16) + _nbytes((n_heads, length, length), BF16),
        temps=_nbytes((tm, d), BF16) + 3 * _nbytes((tm, d), F32))
    cur = lambda i: (jnp.minimum(i, n_blocks - 1), 0)
    lag = lambda i: (jnp.maximum(i - 1, 0), 0)
    return pl.pallas_call(
        kern,
        grid=(n_blocks + 1,),
        in_specs=[
            pl.BlockSpec((tm, d), cur),
            pl.BlockSpec((tm, d), lag),
            pl.BlockSpec((None, 1, d), lambda i: (layer * 3 + 1, 0, 0)),
            _mod_spec(layer, N_MOD + 0, d),
            _mod_spec(layer, N_MOD + 1, d),
            _mod_spec(layer, N_MOD + 2, d),
            pl.BlockSpec((d, 2 * e), lambda i: (0, 0), pipeline_mode=pl.Buffered(1)),
            pl.BlockSpec((None, 1, e), lambda i: (jm, 0, 0)),
            pl.BlockSpec((None, 1, e), lambda i: (jm, 0, 0)),
            pl.BlockSpec((None, n_heads, length, length), lambda i: (jm, 0, 0, 0)),
            pl.BlockSpec((None, n_heads, length, 1), lambda i: (jm, 0, 0, 0)),
            pl.BlockSpec((e, d), lambda i: (0, 0), pipeline_mode=pl.Buffered(1)),
        ],
        out_specs=pl.BlockSpec((tm, d), lag),
        out_shape=jax.ShapeDtypeStruct((m, d), F32),
        scratch_shapes=[pltpu.VMEM((tm, e), F32), pltpu.VMEM((tm, e), F32), pltpu.VMEM((tm, e), BF16),
                        pltpu.VMEM((tm, e), BF16), pltpu.VMEM((n_heads, length, length), BF16)],
        compiler_params=pltpu.CompilerParams(dimension_semantics=("arbitrary",), vmem_limit_bytes=vmem),
        name="gmlp_block",
    )(x2d, x2d, norm_g3, mods, mods, mods, w_in, ln_g[:, None, :], ln_b[:, None, :], ws, bs[..., None], w_out)


def kernel(x, c, ada_w, ada_b, norm_g, ffn_w_in, ffn_w_out, gm_w_in, gm_ln_g, gm_ln_b, gm_ws, gm_bs, gm_w_out,
           cv_w_in, cv_b_in, cv_dw_w, cv_dw_b, cv_ln_g, cv_ln_b, cv_w_out, cv_b_out, final_g):
    bsz, seq, d = x.shape
    depth, n_sub = norm_g.shape[0], norm_g.shape[1]
    n_mixers = 2
    assert n_sub == 3 and ada_w.shape[2] == n_sub * N_MOD * d

    mods = _ada_table(c, ada_w, ada_b)
    norm_g3 = norm_g.reshape(depth * n_sub, 1, d)

    h = x.reshape(bsz * seq, d)
    for i in range(depth):
        jm = i // n_mixers
        is_gmlp = i % n_mixers == 0
        mix_in, mix_out = (gm_w_in, gm_w_out) if is_gmlp else (cv_w_in, cv_w_out)
        h, w_in_b, w_out_b = _ffn(h, mods, norm_g3, ffn_w_in, ffn_w_out, i, 0, 0, seq,
                                  casts=((mix_in, (jm,)), (mix_out, (jm,))))
        if is_gmlp:
            h = _gmlp_block(h, mods, norm_g3, w_in_b, gm_ln_g, gm_ln_b, gm_ws, gm_bs, w_out_b, i, jm, seq)
        else:
            h = _conv_block(h, mods, norm_g3, w_in_b, cv_b_in, cv_dw_w, cv_dw_b, cv_ln_g, cv_ln_b, w_out_b,
                            cv_b_out, i, jm, seq)
        h = _ffn(h, mods, norm_g3, ffn_w_in, ffn_w_out, i, 2, 1, seq,
                 final_g=final_g if i == depth - 1 else None)
    return h.reshape(bsz, seq, d)
```
